```python
import math
import jax, jax.numpy as jnp
from jax import lax
import numpy as np

D_MODEL = 2048
BATCH = 4
SEQ = 2048
DEPTH = 4
DEC_BATCH = 8
DEC_SEQ = 8
PAST_LEN = 16384
PAGE_SIZE = 128

HEAD_DIM = 128
N_MEM_HEADS = 4
MEM_WIDTH = N_MEM_HEADS * HEAD_DIM
SELF_WIDTH = D_MODEL - MEM_WIDTH
N_HEADS = SELF_WIDTH // HEAD_DIM
KV_WIDTH = SELF_WIDTH
IN_WIDTH = 3 * SELF_WIDTH + MEM_WIDTH
DIFF_HEAD_DIM = HEAD_DIM // 2
N_MEM = 256
MOBA_BLOCK = 256
MOBA_TOPK = 3
D_FF = 256 * ((8 * D_MODEL // 3 + 255) // 256)
CONV_W = 3
ROPE_THETA = 10000.0
EPS = 1e-6
SUBLN_EPS = 1e-5
N_MIXERS = 2
N_DIFF = DEPTH // 2
Q_CHUNK_MOBA = 16
Q_CHUNK_DENSE = 128

kernel_name = 'moba_diffattn_memxattn_convffn_step'


def _rmsnorm(x, g, eps=EPS):
    xf = x.astype(jnp.float32)
    r = xf * lax.rsqrt(jnp.mean(xf * xf, axis=-1, keepdims=True) + eps)
    return (r * g.astype(jnp.float32)).astype(x.dtype)


def _rope(x, pos):
    d = x.shape[-1]
    half = d // 2
    inv = jnp.exp(-math.log(ROPE_THETA) * jnp.arange(half, dtype=jnp.float32) / half)
    ang = pos.astype(jnp.float32)[:, None] * inv[None, :]
    cos = jnp.cos(ang)[None, :, None, :]
    sin = jnp.sin(ang)[None, :, None, :]
    xf = x.astype(jnp.float32)
    x1, x2 = xf[..., :half], xf[..., half:]
    return jnp.concatenate([x1 * cos - x2 * sin, x2 * cos + x1 * sin], axis=-1).astype(x.dtype)


def _map_query_chunks(fn, q, q_pos, chunk):
    B, T = q.shape[:2]
    c = min(chunk, T)
    n = -(-T // c)
    pad = n * c - T
    if pad:
        q = jnp.pad(q, [(0, 0), (0, pad)] + [(0, 0)] * (q.ndim - 2))
        q_pos = jnp.concatenate([q_pos, jnp.full((pad,), q_pos[-1], q_pos.dtype)])
    qs = jnp.moveaxis(q.reshape((B, n, c) + q.shape[2:]), 1, 0)
    ps = q_pos.reshape(n, c)
    out = lax.map(lambda a: fn(a[0], a[1]), (qs, ps))
    out = jnp.moveaxis(out, 0, 1)
    out = out.reshape((B, n * c) + out.shape[3:])
    return out[:, :T]


def _moba_attention(q, k, v, q_pos):
    B, L, H, dh = k.shape
    nb = max(-(-L // MOBA_BLOCK), MOBA_TOPK)
    pad = nb * MOBA_BLOCK - L
    kb = jnp.pad(k, ((0, 0), (0, pad), (0, 0), (0, 0))).reshape(B, nb, MOBA_BLOCK, H, dh).transpose(0, 3, 1, 2, 4)
    vb = jnp.pad(v, ((0, 0), (0, pad), (0, 0), (0, 0))).reshape(B, nb, MOBA_BLOCK, H, dh).transpose(0, 3, 1, 2, 4)
    means = jnp.mean(kb.astype(jnp.float32), axis=3)
    blk = jnp.arange(nb, dtype=jnp.int32)
    offs = jnp.arange(MOBA_BLOCK, dtype=jnp.int32)
    own_slot = jnp.arange(MOBA_TOPK + 1) == MOBA_TOPK
    gather = jax.vmap(jax.vmap(lambda blocks, ids: blocks[ids]))
    scale = dh ** -0.5

    def chunk(qc, pc):
        C = qc.shape[1]
        cur = pc // MOBA_BLOCK
        gate = jnp.einsum('bchd,bhnd->bhcn', qc.astype(jnp.float32), means)
        gate = jnp.where(blk[None, None, None, :] < cur[None, None, :, None], gate, -jnp.inf)
        top_val, top_idx = lax.top_k(gate, MOBA_TOPK)
        own = jnp.broadcast_to(cur[None, None, :, None], (B, H, C, 1)).astype(jnp.int32)
        idx = jnp.concatenate([top_idx.astype(jnp.int32), own], axis=-1)
        valid = jnp.concatenate([jnp.isfinite(top_val), jnp.ones((B, H, C, 1), bool)], axis=-1)
        kpos = idx[..., None] * MOBA_BLOCK + offs
        allowed = valid[..., None] & (~own_slot[:, None] | (kpos <= pc[None, None, :, None, None]))
        kg = gather(kb, idx)
        vg = gather(vb, idx)
        s = jnp.einsum('bchd,bhcskd->bhcsk', qc, kg).astype(jnp.float32) * scale
        s = jnp.where(allowed, s, -jnp.inf).reshape(B, H, C, -1)
        p = jax.nn.softmax(s, axis=-1).astype(v.dtype)
        return jnp.einsum('bhcj,bhcjd->bchd', p, vg.reshape(B, H, C, -1, dh))

    return _map_query_chunks(chunk, q, q_pos, Q_CHUNK_MOBA)


def _diff_attention(q, k, v, q_pos, lam):
    B, L, H2, dd = k.shape
    H = H2 // 2
    k_pos = jnp.arange(L, dtype=jnp.int32)
    scale = dd ** -0.5

    def chunk(qc, pc):
        C = qc.shape[1]
        s = jnp.einsum('bchd,blhd->bhcl', qc, k).astype(jnp.float32) * scale
        s = jnp.where(k_pos[None, None, None, :] <= pc[None, None, :, None], s, -jnp.inf)
        p = jax.nn.softmax(s, axis=-1).reshape(B, H, 2, C, L)
        a = (p[:, :, 0] - lam * p[:, :, 1]).astype(v.dtype)
        return jnp.einsum('bhcl,blhe->bche', a, v)

    return _map_query_chunks(chunk, q, q_pos, Q_CHUNK_DENSE)


def _memory_attention(qm, mk, mv):
    s = jnp.einsum('bthd,bmhd->bhtm', qm, mk).astype(jnp.float32) * HEAD_DIM ** -0.5
    p = jax.nn.softmax(s, axis=-1).astype(mv.dtype)
    return jnp.einsum('bhtm,bmhd->bthd', p, mv)


def _memory_kv(mem, ln_mem_i, w_mem_kv_i):
    kv = jnp.einsum('bmd,de->bme', _rmsnorm(mem, ln_mem_i), w_mem_kv_i)
    return kv[..., :MEM_WIDTH], kv[..., MEM_WIDTH:]


def _layer(i, x, q_pos, past_k, past_v, mem_k, mem_v, conv_state,
           ln_attn, w_in, w_o, diff_lambda, diff_subln, ln_ffn, w_up, conv_w, conv_b, w_down):
    B, T, _ = x.shape
    h = _rmsnorm(x, ln_attn[i])
    proj = jnp.einsum('btd,de->bte', h, w_in[i])
    q, k, v, qm = jnp.split(proj, [SELF_WIDTH, 2 * SELF_WIDTH, 3 * SELF_WIDTH], axis=-1)
    if i % N_MIXERS == 0:
        qh = _rope(q.reshape(B, T, N_HEADS, HEAD_DIM), q_pos)
        kh = _rope(k.reshape(B, T, N_HEADS, HEAD_DIM), q_pos)
        k_rows = kh.reshape(B, T, KV_WIDTH)
        k_full = k_rows if past_k is None else jnp.concatenate([past_k, k_rows], axis=1)
        v_full = v if past_v is None else jnp.concatenate([past_v, v], axis=1)
        L = k_full.shape[1]
        self_out = _moba_attention(qh, k_full.reshape(B, L, N_HEADS, HEAD_DIM),
                                   v_full.reshape(B, L, N_HEADS, HEAD_DIM), q_pos)
    else:
        j = i // N_MIXERS
        lam_init = 0.8 - 0.6 * math.exp(-0.3 * i)
        qh = _rope(q.reshape(B, T, 2 * N_HEADS, DIFF_HEAD_DIM), q_pos)
        kh = _rope(k.reshape(B, T, 2 * N_HEADS, DIFF_HEAD_DIM), q_pos)
        k_rows = kh.reshape(B, T, KV_WIDTH)
        k_full = k_rows if past_k is None else jnp.concatenate([past_k, k_rows], axis=1)
        v_full = v if past_v is None else jnp.concatenate([past_v, v], axis=1)
        L = k_full.shape[1]
        lp = diff_lambda[j].astype(jnp.float32)
        lam = jnp.exp(jnp.sum(lp[0] * lp[1])) - jnp.exp(jnp.sum(lp[2] * lp[3])) + lam_init
        o = _diff_attention(qh, k_full.reshape(B, L, 2 * N_HEADS, DIFF_HEAD_DIM),
                            v_full.reshape(B, L, N_HEADS, HEAD_DIM), q_pos, lam)
        self_out = _rmsnorm(o, diff_subln[j], SUBLN_EPS) * (1.0 - lam_init)
    mem_out = _memory_attention(qm.reshape(B, T, N_MEM_HEADS, HEAD_DIM),
                                mem_k.reshape(B, -1, N_MEM_HEADS, HEAD_DIM),
                                mem_v.reshape(B, -1, N_MEM_HEADS, HEAD_DIM))
    heads = jnp.concatenate([self_out.reshape(B, T, SELF_WIDTH), mem_out.reshape(B, T, MEM_WIDTH)], axis=-1)
    x = x + jnp.einsum('bte,ed->btd', heads, w_o[i])
    h2 = _rmsnorm(x, ln_ffn[i])
    u = jnp.einsum('btd,df->btf', h2, w_up[i])
    g, a = u[..., :D_FF], u[..., D_FF:]
    g_ext = jnp.concatenate([conv_state, g], axis=1)
    c = conv_b[i] + sum(conv_w[i, t] * g_ext[:, t:t + T] for t in range(CONV_W))
    new_conv = g_ext[:, -(CONV_W - 1):]
    x = x + jnp.einsum('btf,fd->btd', jax.nn.silu(c) * a, w_down[i])
    return x, k_rows, v, new_conv


def setup_inputs(seed: int = 0) -> dict:
    key = jax.random.key(seed)
    ks = jax.random.split(key, 24)
    f32 = jnp.float32
    n_pages = PAST_LEN // PAGE_SIZE
    n_used = DEC_BATCH * n_pages
    n_pool = n_used + max(n_used // 4, 1)
    perm = jax.random.permutation(ks[0], n_pool)[:n_used]
    page_table = perm.reshape(DEC_BATCH, n_pages).astype(jnp.int32)
    nrm = lambda k, shape, s=1.0: jax.random.normal(k, shape, f32) * s
    gain = lambda k, shape: 1.0 + 0.02 * jax.random.normal(k, shape, f32)
    return {
        'x_prompt': nrm(ks[1], (BATCH, SEQ, D_MODEL)),
        'x_sample': nrm(ks[2], (DEC_BATCH, DEC_SEQ, D_MODEL)),
        'mem_prompt': nrm(ks[3], (BATCH, N_MEM, D_MODEL)),
        'cache_k': nrm(ks[4], (DEPTH, n_pool, PAGE_SIZE, KV_WIDTH)),
        'cache_v': nrm(ks[5], (DEPTH, n_pool, PAGE_SIZE, KV_WIDTH)),
        'cache_mem_k': nrm(ks[6], (DEPTH, DEC_BATCH, N_MEM, MEM_WIDTH)),
        'cache_mem_v': nrm(ks[7], (DEPTH, DEC_BATCH, N_MEM, MEM_WIDTH)),
        'state_conv': nrm(ks[8], (DEPTH, DEC_BATCH, CONV_W - 1, D_FF)),
        'page_table': page_table,
        'ln_attn': gain(ks[9], (DEPTH, D_MODEL)),
        'w_in': nrm(ks[10], (DEPTH, D_MODEL, IN_WIDTH), D_MODEL ** -0.5),
        'w_o': nrm(ks[11], (DEPTH, D_MODEL, D_MODEL), D_MODEL ** -0.5),
        'diff_lambda': nrm(ks[12], (N_DIFF, 4, DIFF_HEAD_DIM), 0.1),
        'diff_subln': gain(ks[13], (N_DIFF, HEAD_DIM)),
        'ln_mem': gain(ks[14], (DEPTH, D_MODEL)),
        'w_mem_kv': nrm(ks[15], (DEPTH, D_MODEL, 2 * MEM_WIDTH), D_MODEL ** -0.5),
        'ln_ffn': gain(ks[16], (DEPTH, D_MODEL)),
        'w_up': nrm(ks[17], (DEPTH, D_MODEL, 2 * D_FF), D_MODEL ** -0.5),
        'conv_w': nrm(ks[18], (DEPTH, CONV_W, D_FF), CONV_W ** -0.5),
        'conv_b': nrm(ks[19], (DEPTH, D_FF), 0.01),
        'w_down': nrm(ks[20], (DEPTH, D_FF, D_MODEL), D_FF ** -0.5),
        'ln_final': gain(ks[21], (D_MODEL,)),
    }


def reference(x_prompt, x_sample, mem_prompt, cache_k, cache_v, cache_mem_k, cache_mem_v, state_conv,
              page_table, ln_attn, w_in, w_o, diff_lambda, diff_subln, ln_mem, w_mem_kv, ln_ffn,
              w_up, conv_w, conv_b, w_down, ln_final):
    bp, tp = x_prompt.shape[:2]
    bs, ts = x_sample.shape[:2]
    past = page_table.shape[1] * cache_k.shape[2]
    pos_p = jnp.arange(tp, dtype=jnp.int32)
    pos_s = past + jnp.arange(ts, dtype=jnp.int32)
    conv0 = jnp.zeros((bp, CONV_W - 1, D_FF), x_prompt.dtype)
    xp, xs = x_prompt, x_sample
    kp_l, vp_l, mkp_l, mvp_l, cp_l, ks_l, vs_l, cs_l = [], [], [], [], [], [], [], []
    for i in range(DEPTH):
        mk_p, mv_p = _memory_kv(mem_prompt, ln_mem[i], w_mem_kv[i])
        xp, kp, vp, cp = _layer(i, xp, pos_p, None, None, mk_p, mv_p, conv0,
                                ln_attn, w_in, w_o, diff_lambda, diff_subln, ln_ffn, w_up, conv_w, conv_b, w_down)
        past_k = cache_k[i, page_table].reshape(bs, past, KV_WIDTH)
        past_v = cache_v[i, page_table].reshape(bs, past, KV_WIDTH)
        xs, k_s, v_s, c_s = _layer(i, xs, pos_s, past_k, past_v, cache_mem_k[i], cache_mem_v[i], state_conv[i],
                                   ln_attn, w_in, w_o, diff_lambda, diff_subln, ln_ffn, w_up, conv_w, conv_b, w_down)
        kp_l.append(kp); vp_l.append(vp); mkp_l.append(mk_p); mvp_l.append(mv_p); cp_l.append(cp)
        ks_l.append(k_s); vs_l.append(v_s); cs_l.append(c_s)
    y_prompt = _rmsnorm(xp, ln_final)
    y_sample = _rmsnorm(xs, ln_final)
    return (y_prompt, y_sample,
            jnp.stack(kp_l), jnp.stack(vp_l), jnp.stack(mkp_l), jnp.stack(mvp_l), jnp.stack(cp_l),
            jnp.stack(ks_l), jnp.stack(vs_l), jnp.stack(cs_l))
```

```python
import functools
import math

import jax
import jax.numpy as jnp
from jax import lax
from jax.experimental import pallas as pl
from jax.experimental.pallas import tpu as pltpu

F32 = jnp.float32
BF16 = jnp.bfloat16

HEAD_DIM = 128
N_MEM_HEADS = 4
MOBA_BLOCK = 256
MOBA_TOPK = 3
CONV_W = 3
ROPE_THETA = 10000.0
EPS = 1e-6
SUBLN_EPS = 1e-5
N_MIXERS = 2

VMEM_LIMIT_BYTES = 52 * 1024 * 1024
NEG_INF = float("-inf")
BF16_SUBLANES = 16


def _params(*sem):
    return pltpu.CompilerParams(dimension_semantics=sem, vmem_limit_bytes=VMEM_LIMIT_BYTES)


def _pick(n, prefs):
    for p in prefs:
        if n % p == 0:
            return p
    return n


def _norm_matmul_kernel(x_ref, g_ref, w_ref, o_ref, h_ref):
    @pl.when(pl.program_id(1) == 0)
    def _():
        x = x_ref[...]
        ms = jnp.mean(x * x, axis=-1, keepdims=True)
        h_ref[...] = (x * lax.rsqrt(ms + EPS) * g_ref[...]).astype(BF16)

    o_ref[...] = jnp.dot(h_ref[...], w_ref[...], preferred_element_type=F32)


def _norm_matmul(x, g, w):
    m, d = x.shape
    n = w.shape[1]
    tm = _pick(m, (1024, 512, 256, 128, 64))
    tn = _pick(n, (512, 256, 128))
    return pl.pallas_call(
        _norm_matmul_kernel,
        grid=(m // tm, n // tn),
        in_specs=[
            pl.BlockSpec((tm, d), lambda i, j: (i, 0)),
            pl.BlockSpec((1, d), lambda i, j: (0, 0)),
            pl.BlockSpec((d, tn), lambda i, j: (0, j)),
        ],
        out_specs=pl.BlockSpec((tm, tn), lambda i, j: (i, j)),
        out_shape=jax.ShapeDtypeStruct((m, n), F32),
        scratch_shapes=[pltpu.VMEM((tm, d), BF16)],
        compiler_params=_params("parallel", "arbitrary"),
        name="norm_matmul",
    )(x, g.reshape(1, d), w)


def _matmul_res_kernel(a_ref, w_ref, r_ref, o_ref):
    o_ref[...] = r_ref[...] + jnp.dot(a_ref[...], w_ref[...], preferred_element_type=F32)


def _matmul_res(a, w, res):
    m, k = a.shape
    n = w.shape[1]
    tm = _pick(m, (512, 256, 128, 64))
    tn = _pick(n, (512, 256, 128))
    return pl.pallas_call(
        _matmul_res_kernel,
        grid=(m // tm, n // tn),
        in_specs=[
            pl.BlockSpec((tm, k), lambda i, j: (i, 0)),
            pl.BlockSpec((k, tn), lambda i, j: (0, j)),
            pl.BlockSpec((tm, tn), lambda i, j: (i, j)),
        ],
        out_specs=pl.BlockSpec((tm, tn), lambda i, j: (i, j)),
        out_shape=jax.ShapeDtypeStruct((m, n), F32),
        compiler_params=_params("parallel", "parallel"),
        name="matmul_res",
    )(a, w, res)


def _rmsnorm_kernel(x_ref, g_ref, o_ref):
    x = x_ref[...]
    ms = jnp.mean(x * x, axis=-1, keepdims=True)
    o_ref[...] = x * lax.rsqrt(ms + EPS) * g_ref[...]


def _rmsnorm(x, g):
    m, d = x.shape
    tm = _pick(m, (512, 256, 128, 64))
    return pl.pallas_call(
        _rmsnorm_kernel,
        grid=(m // tm,),
        in_specs=[pl.BlockSpec((tm, d), lambda i: (i, 0)), pl.BlockSpec((1, d), lambda i: (0, 0))],
        out_specs=pl.BlockSpec((tm, d), lambda i: (i, 0)),
        out_shape=jax.ShapeDtypeStruct((m, d), F32),
        compiler_params=_params("parallel"),
        name="rmsnorm",
    )(x, g.reshape(1, d))


def _rope_tables(pos, head_dim):
    half = head_dim // 2
    inv = jnp.exp(-math.log(ROPE_THETA) * jnp.arange(half, dtype=F32) / half)
    ang = pos.astype(F32)[:, None] * inv[None, :]
    cos, sin = jnp.cos(ang), jnp.sin(ang)
    reps = HEAD_DIM // head_dim
    cos_t = jnp.tile(jnp.concatenate([cos, cos], axis=-1), (1, reps))
    sin_t = jnp.tile(jnp.concatenate([-sin, sin], axis=-1), (1, reps))
    return cos_t, sin_t


def _rope_kernel(q_ref, k_ref, v_ref, cos_ref, sin_ref, *out_refs, head_dim, with_bf16, with_means):
    qf_ref, kf_ref, vf_ref = out_refs[:3]
    if with_bf16:
        qb_ref, kb_ref, vb_ref = out_refs[3:6]
    cos = cos_ref[...]
    sin = sin_ref[...]
    half = head_dim // 2
    n_groups = q_ref.shape[1] // HEAD_DIM
    if head_dim != HEAD_DIM:
        lane = lax.broadcasted_iota(jnp.int32, cos.shape, 1)
        first_half = (lane % head_dim) < half

    def rot(x):
        if head_dim == HEAD_DIM:
            partner = pltpu.roll(x, half, 1)
        else:
            partner = jnp.where(first_half, pltpu.roll(x, HEAD_DIM - half, 1), pltpu.roll(x, half, 1))
        return x * cos + partner * sin

    for h in range(n_groups):
        sl = slice(h * HEAD_DIM, (h + 1) * HEAD_DIM)
        qr = rot(q_ref[:, sl])
        kr = rot(k_ref[:, sl])
        qf_ref[:, sl] = qr
        kf_ref[:, sl] = kr
        if with_bf16:
            qb_ref[:, sl] = qr.astype(BF16)
            kb_ref[:, sl] = kr.astype(BF16)
        if with_means:
            out_refs[-1][:, sl] = jnp.sum(kr, axis=0, keepdims=True) * (1.0 / MOBA_BLOCK)
    v = v_ref[...]
    vf_ref[...] = v
    if with_bf16:
        vb_ref[...] = v.astype(BF16)


def _rope(proj, cos_t, sin_t, *, seq, width, head_dim, with_bf16, with_means):
    m = proj.shape[0]
    tr = MOBA_BLOCK if seq % MOBA_BLOCK == 0 else seq
    n_t = seq // tr
    if with_means:
        assert tr == MOBA_BLOCK
    row = lambda c: pl.BlockSpec((tr, width), lambda r: (r, c))
    tab = pl.BlockSpec((tr, HEAD_DIM), lambda r: (r % n_t, 0))
    out_shapes = [jax.ShapeDtypeStruct((m, width), F32)] * 3
    if with_bf16:
        out_shapes += [jax.ShapeDtypeStruct((m, width), BF16)] * 3
    out_specs = [row(0)] * len(out_shapes)
    if with_means:
        out_shapes.append(jax.ShapeDtypeStruct((m // tr, 1, width), F32))
        out_specs.append(pl.BlockSpec((None, 1, width), lambda r: (r, 0, 0)))
    return pl.pallas_call(
        functools.partial(_rope_kernel, head_dim=head_dim, with_bf16=with_bf16, with_means=with_means),
        grid=(m // tr,),
        in_specs=[row(0), row(1), row(2), tab, tab],
        out_specs=out_specs,
        out_shape=out_shapes,
        compiler_params=_params("parallel"),
        name="rope",
    )(proj, proj, proj, cos_t, sin_t)


def _nt_dot(a, b):
    return lax.dot_general(a, b, (((1,), (1,)), ((), ())), preferred_element_type=F32)


def _flash_prompt(q, k_ref, v_ref, qi, scale, past_bias_fn):
    blk = MOBA_BLOCK
    rows = q.shape[0]
    own = pl.multiple_of(qi * blk, blk)
    s = _nt_dot(q, k_ref[pl.ds(own, blk), :]) * scale
    r_id = lax.broadcasted_iota(jnp.int32, s.shape, 0) % blk
    c_id = lax.broadcasted_iota(jnp.int32, s.shape, 1)
    s = jnp.where(c_id <= r_id, s, NEG_INF)
    m0 = jnp.max(s, axis=1, keepdims=True)
    p = jnp.exp(s - m0)
    l0 = jnp.sum(p, axis=1, keepdims=True)
    acc0 = jnp.dot(p.astype(BF16), v_ref[pl.ds(own, blk), :], preferred_element_type=F32)

    def body(kj, carry):
        m, l, acc = carry
        off = pl.multiple_of(kj * blk, blk)
        s = _nt_dot(q, k_ref[pl.ds(off, blk), :]) * scale
        bias = past_bias_fn(kj)
        if bias is not None:
            s = s + bias
        m_new = jnp.maximum(m, jnp.max(s, axis=1, keepdims=True))
        alpha = jnp.exp(m - m_new)
        p = jnp.exp(s - m_new)
        l = alpha * l + jnp.sum(p, axis=1, keepdims=True)
        acc = alpha * acc + jnp.dot(p.astype(BF16), v_ref[pl.ds(off, blk), :], preferred_element_type=F32)
        return m_new, l, acc

    _, l, acc = lax.fori_loop(0, qi, body, (m0, l0, acc0))
    del rows
    return acc, l


def _moba_prompt_kernel(qb_ref, qf_ref, k_ref, v_ref, mean_ref, o_ref):
    qi = pl.program_id(2)
    nb = mean_ref.shape[0]
    gate = lax.dot_general(qf_ref[...], mean_ref[...], (((1,), (1,)), ((), ())),
                           precision=lax.Precision.HIGHEST, preferred_element_type=F32)
    blk_id = lax.broadcasted_iota(jnp.int32, gate.shape, 1)
    rank = jnp.zeros(gate.shape, F32)
    for m in range(nb):
        gm = gate[:, m:m + 1]
        beats = (gm > gate) | ((gm == gate) & (m < blk_id))
        rank = rank + jnp.where(beats, 1.0, 0.0) * jnp.where(m < qi, 1.0, 0.0)
    selected = (blk_id < qi) & (rank < float(MOBA_TOPK))
    sel_bias = jnp.where(selected, 0.0, NEG_INF)

    def past_bias(kj):
        return jnp.min(jnp.where(blk_id == kj, sel_bias, 0.0), axis=1, keepdims=True)

    acc, l = _flash_prompt(qb_ref[...], k_ref, v_ref, qi, HEAD_DIM ** -0.5, past_bias)
    o_ref[...] = (acc / l).astype(o_ref.dtype)


def _moba_prompt(qb, qf, kb, vb, means, *, batch, seq):
    m, width = qb.shape
    n_heads = width // HEAD_DIM
    nq = seq // MOBA_BLOCK
    nb = means.shape[1]
    q_spec = pl.BlockSpec((MOBA_BLOCK, HEAD_DIM), lambda b, h, i: (b * nq + i, h))
    kv_spec = pl.BlockSpec((seq, HEAD_DIM), lambda b, h, i: (b, h))
    return pl.pallas_call(
        _moba_prompt_kernel,
        grid=(batch, n_heads, nq),
        in_specs=[q_spec, q_spec, kv_spec, kv_spec,
                  pl.BlockSpec((None, nb, HEAD_DIM), lambda b, h, i: (b, 0, h))],
        out_specs=q_spec,
        out_shape=jax.ShapeDtypeStruct((m, width), BF16),
        compiler_params=_params("parallel", "parallel", "arbitrary"),
        name="moba_prompt",
    )(qb, qf, kb, vb, means)


def _diff_lambda(lp_ref):
    lp = lp_ref[...]
    s1 = jnp.sum(lp[0:1, :] * lp[1:2, :], axis=1, keepdims=True)
    s2 = jnp.sum(lp[2:3, :] * lp[3:4, :], axis=1, keepdims=True)
    return jnp.exp(s1) - jnp.exp(s2)


def _diff_finish(o1, o2, lam, subln, lam_init):
    o = o1 - lam * o2
    ms = jnp.mean(o * o, axis=-1, keepdims=True)
    return (o * lax.rsqrt(ms + SUBLN_EPS) * subln) * (1.0 - lam_init)


def _diff_prompt_kernel(qb_ref, k_ref, v_ref, lp_ref, subln_ref, o_ref, *, lam_init):
    qi = pl.program_id(2)
    blk = MOBA_BLOCK
    dd = HEAD_DIM // 2
    q = qb_ref[...].astype(F32)
    lane = lax.broadcasted_iota(jnp.int32, q.shape, 1)
    q_stack = jnp.concatenate([jnp.where(lane < dd, q, 0.0), jnp.where(lane >= dd, q, 0.0)], axis=0).astype(BF16)
    acc, l = _flash_prompt(q_stack, k_ref, v_ref, qi, dd ** -0.5, lambda kj: None)
    o = acc / l
    lam = _diff_lambda(lp_ref) + lam_init
    o_ref[...] = _diff_finish(o[:blk], o[blk:], lam, subln_ref[...], lam_init).astype(o_ref.dtype)


def _diff_prompt(qb, kb, vb, lp, subln, *, batch, seq, lam_init):
    m, width = qb.shape
    n_heads = width // HEAD_DIM
    nq = seq // MOBA_BLOCK
    q_spec = pl.BlockSpec((MOBA_BLOCK, HEAD_DIM), lambda b, h, i: (b * nq + i, h))
    kv_spec = pl.BlockSpec((seq, HEAD_DIM), lambda b, h, i: (b, h))
    return pl.pallas_call(
        functools.partial(_diff_prompt_kernel, lam_init=lam_init),
        grid=(batch, n_heads, nq),
        in_specs=[q_spec, kv_spec, kv_spec,
                  pl.BlockSpec(lp.shape, lambda b, h, i: (0, 0)),
                  pl.BlockSpec((1, HEAD_DIM), lambda b, h, i: (0, 0))],
        out_specs=q_spec,
        out_shape=jax.ShapeDtypeStruct((m, width), BF16),
        compiler_params=_params("parallel", "parallel", "arbitrary"),
        name="diff_prompt",
    )(qb, kb, vb, lp, subln.reshape(1, HEAD_DIM))


def _pad_rows(x, rows):
    if x.shape[0] >= rows:
        return x
    return jnp.concatenate([x, jnp.zeros((rows - x.shape[0],) + x.shape[1:], x.dtype)], axis=0)


def _mem_attn_kernel(q_ref, mk_ref, mv_ref, o_ref):
    tq = q_ref.shape[0]
    q = _pad_rows(q_ref[...], BF16_SUBLANES).astype(BF16)
    s = _nt_dot(q, mk_ref[...].astype(BF16)) * (HEAD_DIM ** -0.5)
    m = jnp.max(s, axis=1, keepdims=True)
    p = jnp.exp(s - m)
    l = jnp.sum(p, axis=1, keepdims=True)
    o = jnp.dot(p.astype(BF16), mv_ref[...].astype(BF16), preferred_element_type=F32)
    o_ref[...] = (o / l)[:tq].astype(o_ref.dtype)


def _mem_attn(proj, mem_k, mem_v, *, batch, seq, q_col_block):
    m = proj.shape[0]
    n_mem, mem_width = mem_k.shape[1:]
    n_h = mem_width // HEAD_DIM
    tq = _pick(seq, (512, 256, 128, 64, 32, 16, 8))
    nq = seq // tq
    kv_spec = pl.BlockSpec((None, n_mem, HEAD_DIM), lambda b, i, h: (b, 0, h))
    return pl.pallas_call(
        _mem_attn_kernel,
        grid=(batch, nq, n_h),
        in_specs=[pl.BlockSpec((tq, HEAD_DIM), lambda b, i, h: (b * nq + i, q_col_block + h)), kv_spec, kv_spec],
        out_specs=pl.BlockSpec((tq, HEAD_DIM), lambda b, i, h: (b * nq + i, h)),
        out_shape=jax.ShapeDtypeStruct((m, mem_width), BF16 if tq % BF16_SUBLANES == 0 else F32),
        compiler_params=_params("parallel", "parallel", "parallel"),
        name="mem_attn",
    )(proj, mem_k, mem_v)


def _conv_gate_kernel(g_ref, a_ref, st_ref, w_ref, b_ref, act_ref, nst_ref):
    g = g_ref[...]
    seq = g.shape[0]
    st = st_ref[...]
    w = w_ref[...]
    row = lax.broadcasted_iota(jnp.int32, g.shape, 0)
    g1 = jnp.where(row == 0, st[1:2, :], pltpu.roll(g, 1, 0))
    g2 = jnp.where(row == 0, st[0:1, :], jnp.where(row == 1, st[1:2, :], pltpu.roll(g, 2, 0)))
    c = b_ref[...] + w[0:1, :] * g2 + w[1:2, :] * g1 + w[2:3, :] * g
    act_ref[...] = (c / (1.0 + jnp.exp(-c)) * a_ref[...]).astype(act_ref.dtype)
    nst_ref[...] = g[seq - (CONV_W - 1):, :]


def _conv_gate(u, state, conv_w, conv_b, *, batch, seq):
    d_ff = u.shape[1] // 2
    assert seq >= CONV_W - 1
    tc = _pick(d_ff, (512, 256, 128)) if seq > 64 else d_ff
    nc = d_ff // tc
    return pl.pallas_call(
        _conv_gate_kernel,
        grid=(batch, nc),
        in_specs=[
            pl.BlockSpec((seq, tc), lambda b, j: (b, j)),
            pl.BlockSpec((seq, tc), lambda b, j: (b, nc + j)),
            pl.BlockSpec((None, CONV_W - 1, tc), lambda b, j: (b, 0, j)),
            pl.BlockSpec((CONV_W, tc), lambda b, j: (0, j)),
            pl.BlockSpec((1, tc), lambda b, j: (0, j)),
        ],
        out_specs=[
            pl.BlockSpec((seq, tc), lambda b, j: (b, j)),
            pl.BlockSpec((None, CONV_W - 1, tc), lambda b, j: (b, 0, j)),
        ],
        out_shape=[
            jax.ShapeDtypeStruct((batch * seq, d_ff), BF16),
            jax.ShapeDtypeStruct((batch, CONV_W - 1, d_ff), F32),
        ],
        compiler_params=_params("parallel", "parallel"),
        name="conv_gate",
    )(u, u, state, conv_w, conv_b.reshape(1, d_ff))


def _page_mean_kernel(pt_ref, k0_ref, k1_ref, o_ref):
    del pt_ref
    s = jnp.sum(k0_ref[...], axis=0, keepdims=True) + jnp.sum(k1_ref[...], axis=0, keepdims=True)
    o_ref[...] = s * (1.0 / MOBA_BLOCK)


def _page_means(cache_k, page_table_flat, layer, *, batch, n_pages):
    page, width = cache_k.shape[2:]
    assert MOBA_BLOCK == 2 * page
    n_blocks = n_pages // 2

    def k_spec(half):
        return pl.BlockSpec((None, None, page, width),
                            lambda b, n, pt: (layer, pt[b * n_pages + 2 * n + half], 0, 0))

    return pl.pallas_call(
        _page_mean_kernel,
        grid_spec=pltpu.PrefetchScalarGridSpec(
            num_scalar_prefetch=1,
            grid=(batch, n_blocks),
            in_specs=[k_spec(0), k_spec(1)],
            out_specs=pl.BlockSpec((None, None, 1, width), lambda b, n, pt: (b, n, 0, 0)),
        ),
        out_shape=jax.ShapeDtypeStruct((batch, n_blocks, 1, width), F32),
        compiler_params=_params("parallel", "parallel"),
        name="page_means",
    )(page_table_flat, cache_k, cache_k)


def _topk_kernel(q_ref, mean_ref, o_ref):
    n_heads = q_ref.shape[1] // HEAD_DIM
    n_blocks = mean_ref.shape[0]
    for h in range(n_heads):
        sl = slice(h * HEAD_DIM, (h + 1) * HEAD_DIM)
        gate = lax.dot_general(q_ref[:, sl], mean_ref[:, sl], (((1,), (1,)), ((), ())),
                               precision=lax.Precision.HIGHEST, preferred_element_type=F32)
        lane = lax.broadcasted_iota(jnp.int32, gate.shape, 1)
        out_lane = lax.broadcasted_iota(jnp.int32, (gate.shape[0], HEAD_DIM), 1)
        out = jnp.zeros((gate.shape[0], HEAD_DIM), jnp.int32)
        for r in range(MOBA_TOPK):
            mx = jnp.max(gate, axis=1, keepdims=True)
            idx = jnp.min(jnp.where(gate == mx, lane, n_blocks), axis=1, keepdims=True)
            out = jnp.where(out_lane == r, idx, out)
            gate = jnp.where(lane == idx, NEG_INF, gate)
        o_ref[h] = out


def _sample_topk(qf, means, *, batch, seq):
    width = qf.shape[1]
    n_heads = width // HEAD_DIM
    n_blocks = means.shape[1]
    return pl.pallas_call(
        _topk_kernel,
        grid=(batch,),
        in_specs=[pl.BlockSpec((seq, width), lambda b: (b, 0)),
                  pl.BlockSpec((None, n_blocks, width), lambda b: (b, 0, 0))],
        out_specs=pl.BlockSpec((None, n_heads, seq, HEAD_DIM), lambda b: (b, 0, 0, 0)),
        out_shape=jax.ShapeDtypeStruct((batch, n_heads, seq, HEAD_DIM), jnp.int32),
        compiler_params=_params("parallel"),
        name="sample_topk",
    )(qf, means)


def _moba_sample_kernel(pt_ref, idx_ref, q_ref, kn_ref, vn_ref, *refs):
    del pt_ref, idx_ref
    n_tiles = 2 * MOBA_TOPK
    k_refs = refs[:n_tiles]
    v_refs = refs[n_tiles:2 * n_tiles]
    o_ref = refs[2 * n_tiles]
    c = pl.program_id(2)
    scale = HEAD_DIM ** -0.5
    seq = q_ref.shape[0]
    page = k_refs[0].shape[0]
    q = _pad_rows(q_ref[...], BF16_SUBLANES).astype(BF16)
    s_own = _nt_dot(q, _pad_rows(kn_ref[...], page).astype(BF16)) * scale
    r_id = lax.broadcasted_iota(jnp.int32, s_own.shape, 0)
    c_id = lax.broadcasted_iota(jnp.int32, s_own.shape, 1)
    s_own = jnp.where(c_id <= r_id, s_own, NEG_INF)
    s_past = [_nt_dot(q, k_ref[...].astype(BF16)) * scale for k_ref in k_refs]
    m = jnp.max(s_own, axis=1, keepdims=True)
    for s in s_past:
        m = jnp.maximum(m, jnp.max(s, axis=1, keepdims=True))
    p_own = jnp.exp(s_own - m)
    l = jnp.sum(p_own, axis=1, keepdims=True)
    acc = jnp.dot(p_own.astype(BF16), _pad_rows(vn_ref[...], page).astype(BF16), preferred_element_type=F32)
    for s, v_ref in zip(s_past, v_refs):
        p = jnp.exp(s - m)
        l = l + jnp.sum(p, axis=1, keepdims=True)
        acc = acc + jnp.dot(p.astype(BF16), v_ref[...].astype(BF16), preferred_element_type=F32)
    res = (acc / l)[:seq]
    row = lax.broadcasted_iota(jnp.int32, res.shape, 0)

    @pl.when(c == 0)
    def _():
        o_ref[...] = jnp.zeros_like(o_ref)

    o_ref[...] = jnp.where(row == c, res, o_ref[...])


def _moba_sample(qf, k_new, v_new, cache_k, cache_v, page_table_flat, idx_flat, layer, *, batch, seq, n_pages):
    width = qf.shape[1]
    n_heads = width // HEAD_DIM
    page = cache_k.shape[2]
    assert n_pages // 2 >= MOBA_TOPK

    def kv_spec(slot, half):
        def imap(b, h, c, pt, idx):
            blk = idx[((b * n_heads + h) * seq + c) * MOBA_TOPK + slot]
            return (layer, pt[b * n_pages + 2 * blk + half], 0, h)
        return pl.BlockSpec((None, None, page, HEAD_DIM), imap)

    tiles = [kv_spec(s, half) for s in range(MOBA_TOPK) for half in range(2)]
    new_spec = pl.BlockSpec((seq, HEAD_DIM), lambda b, h, c, pt, idx: (b, h))
    return pl.pallas_call(
        _moba_sample_kernel,
        grid_spec=pltpu.PrefetchScalarGridSpec(
            num_scalar_prefetch=2,
            grid=(batch, n_heads, seq),
            in_specs=[new_spec, new_spec, new_spec] + tiles + tiles,
            out_specs=new_spec,
        ),
        out_shape=jax.ShapeDtypeStruct((batch * seq, width), F32),
        compiler_params=_params("parallel", "parallel", "arbitrary"),
        name="moba_sample",
    )(page_table_flat, idx_flat, qf, k_new, v_new, *([cache_k] * len(tiles)), *([cache_v] * len(tiles)))


def _diff_sample_kernel(pt_ref, q_ref, kn_ref, vn_ref, k_ref, v_ref, lp_ref, subln_ref, o_ref,
                        qbd_ref, m_ref, l_ref, acc_ref, *, lam_init, past_len):
    del pt_ref
    p_id = pl.program_id(1)
    n_p = pl.num_programs(1)
    seq, width = q_ref.shape
    dd = HEAD_DIM // 2
    n_sub = width // dd
    n_heads = width // HEAD_DIM
    rows = n_sub * seq
    scale = dd ** -0.5

    @pl.when(p_id == 0)
    def _():
        q = q_ref[...]
        q_rep = jnp.concatenate([q] * n_sub, axis=0)
        r_sub = lax.broadcasted_iota(jnp.int32, q_rep.shape, 0) // seq
        c_sub = lax.broadcasted_iota(jnp.int32, q_rep.shape, 1) // dd
        qbd_ref[...] = jnp.where(r_sub == c_sub, q_rep, 0.0).astype(BF16)
        m_ref[...] = jnp.full(m_ref.shape, NEG_INF, F32)
        l_ref[...] = jnp.zeros(l_ref.shape, F32)
        acc_ref[...] = jnp.zeros(acc_ref.shape, F32)

    def update(s, v_bf16):
        m_old = m_ref[...]
        m_new = jnp.maximum(m_old, jnp.max(s, axis=1, keepdims=True))
        alpha = jnp.exp(m_old - m_new)
        p = jnp.exp(s - m_new).astype(BF16)
        l_ref[...] = alpha * l_ref[...] + jnp.sum(p.astype(F32), axis=1, keepdims=True)
        m_ref[...] = m_new
        g = 2 * seq
        for h in range(n_heads):
            pv = jnp.dot(p[h * g:(h + 1) * g, :], v_bf16[:, h * HEAD_DIM:(h + 1) * HEAD_DIM],
                         preferred_element_type=F32)
            acc_ref[h * g:(h + 1) * g, :] = alpha[h * g:(h + 1) * g, :] * acc_ref[h * g:(h + 1) * g, :] + pv

    update(_nt_dot(qbd_ref[...], k_ref[...].astype(BF16)) * scale, v_ref[...].astype(BF16))

    @pl.when(p_id == n_p - 1)
    def _():
        page = k_ref.shape[0]
        s = _nt_dot(qbd_ref[...], _pad_rows(kn_ref[...], page).astype(BF16)) * scale
        q_pos = lax.broadcasted_iota(jnp.int32, s.shape, 0) % seq
        k_pos = lax.broadcasted_iota(jnp.int32, s.shape, 1)
        update(jnp.where(k_pos <= q_pos, s, NEG_INF), _pad_rows(vn_ref[...], page).astype(BF16))
        o = acc_ref[...] / l_ref[...]
        lam = _diff_lambda(lp_ref) + lam_init
        subln = subln_ref[...]
        g = 2 * seq
        for h in range(n_heads):
            o1 = o[h * g:h * g + seq, :]
            o2 = o[h * g + seq:(h + 1) * g, :]
            o_ref[:, h * HEAD_DIM:(h + 1) * HEAD_DIM] = _diff_finish(o1, o2, lam, subln, lam_init)
    del rows, past_len


def _diff_sample(qf, k_new, v_new, cache_k, cache_v, page_table_flat, lp, subln, layer, *,
                 batch, seq, n_pages, lam_init):
    width = qf.shape[1]
    page = cache_k.shape[2]
    dd = HEAD_DIM // 2
    rows = (width // dd) * seq
    new_spec = pl.BlockSpec((seq, width), lambda b, p, pt: (b, 0))
    kv_spec = pl.BlockSpec((None, None, page, width), lambda b, p, pt: (layer, pt[b * n_pages + p], 0, 0))
    return pl.pallas_call(
        functools.partial(_diff_sample_kernel, lam_init=lam_init, past_len=n_pages * page),
        grid_spec=pltpu.PrefetchScalarGridSpec(
            num_scalar_prefetch=1,
            grid=(batch, n_pages),
            in_specs=[new_spec, new_spec, new_spec, kv_spec, kv_spec,
                      pl.BlockSpec(lp.shape, lambda b, p, pt: (0, 0)),
                      pl.BlockSpec((1, HEAD_DIM), lambda b, p, pt: (0, 0))],
            out_specs=new_spec,
            scratch_shapes=[pltpu.VMEM((rows, width), BF16), pltpu.VMEM((rows, 1), F32),
                            pltpu.VMEM((rows, 1), F32), pltpu.VMEM((rows, HEAD_DIM), F32)],
        ),
        out_shape=jax.ShapeDtypeStruct((batch * seq, width), F32),
        compiler_params=_params("parallel", "arbitrary"),
        name="diff_sample",
    )(page_table_flat, qf, k_new, v_new, cache_k, cache_v, lp, subln.reshape(1, HEAD_DIM))


def _layer(i, x, pos, mem_k, mem_v, conv_state, wts, *, batch, seq, paged=None):
    width = wts["self_width"]
    is_moba = i % N_MIXERS == 0
    head_dim = HEAD_DIM if is_moba else HEAD_DIM // 2
    proj = _norm_matmul(x, wts["ln_attn"][i], wts["w_in"][i])
    cos_t, sin_t = _rope_tables(pos, head_dim)
    rope_out = _rope(proj, cos_t, sin_t, seq=seq, width=width, head_dim=head_dim,
                     with_bf16=paged is None, with_means=is_moba and paged is None)
    qf, kf, vf = rope_out[:3]
    if paged is None:
        qb, kb, vb = rope_out[3:6]
    if is_moba:
        if paged is None:
            means = rope_out[6].reshape(batch, seq // MOBA_BLOCK, width)
            self_out = _moba_prompt(qb, qf, kb, vb, means, batch=batch, seq=seq)
        else:
            cache_k, cache_v, pt_flat, n_pages = paged
            means = _page_means(cache_k, pt_flat, i, batch=batch, n_pages=n_pages)
            means = means.reshape(batch, n_pages // 2, width)
            idx = _sample_topk(qf, means, batch=batch, seq=seq)
            self_out = _moba_sample(qf, kf, vf, cache_k, cache_v, pt_flat, idx[..., :MOBA_TOPK].reshape(-1), i,
                                    batch=batch, seq=seq, n_pages=n_pages).astype(BF16)
    else:
        j = i // N_MIXERS
        lam_init = 0.8 - 0.6 * math.exp(-0.3 * i)
        lp = wts["diff_lambda"][j]
        subln = wts["diff_subln"][j]
        if paged is None:
            self_out = _diff_prompt(qb, kb, vb, lp, subln, batch=batch, seq=seq, lam_init=lam_init)
        else:
            cache_k, cache_v, pt_flat, n_pages = paged
            self_out = _diff_sample(qf, kf, vf, cache_k, cache_v, pt_flat, lp, subln, i,
                                    batch=batch, seq=seq, n_pages=n_pages, lam_init=lam_init).astype(BF16)
    mem_out = _mem_attn(proj, mem_k, mem_v, batch=batch, seq=seq, q_col_block=3 * width // HEAD_DIM)
    heads = jnp.concatenate([self_out, mem_out.astype(BF16)], axis=-1)
    x = _matmul_res(heads, wts["w_o"][i], x)
    u = _norm_matmul(x, wts["ln_ffn"][i], wts["w_up"][i])
    act, new_conv = _conv_gate(u, conv_state, wts["conv_w"][i], wts["conv_b"][i], batch=batch, seq=seq)
    x = _matmul_res(act, wts["w_down"][i], x)
    return x, kf, vf, new_conv


def kernel(x_prompt, x_sample, mem_prompt, cache_k, cache_v, cache_mem_k, cache_mem_v, state_conv, page_table,
           ln_attn, w_in, w_o, diff_lambda, diff_subln, ln_mem, w_mem_kv, ln_ffn, w_up, conv_w, conv_b,
           w_down, ln_final):
    bp, tp, d_model = x_prompt.shape
    bs, ts, _ = x_sample.shape
    depth = w_in.shape[0]
    n_mem = mem_prompt.shape[1]
    mem_width = w_mem_kv.shape[2] // 2
    self_width = d_model - mem_width
    d_ff = w_down.shape[1]
    n_pages = page_table.shape[1]
    page = cache_k.shape[2]
    past = n_pages * page
    assert tp % MOBA_BLOCK == 0 and past % MOBA_BLOCK == 0 and ts <= MOBA_BLOCK

    wts = dict(
        self_width=self_width, ln_attn=ln_attn, ln_ffn=ln_ffn, diff_lambda=diff_lambda, diff_subln=diff_subln,
        conv_w=conv_w, conv_b=conv_b,
        w_in=w_in.astype(BF16), w_o=w_o.astype(BF16), w_up=w_up.astype(BF16), w_down=w_down.astype(BF16),
    )
    w_mem_kv_b = w_mem_kv.astype(BF16)
    pos_p = jnp.arange(tp, dtype=jnp.int32)
    pos_s = past + jnp.arange(ts, dtype=jnp.int32)
    conv0 = jnp.zeros((bp, CONV_W - 1, d_ff), F32)
    pt_flat = page_table.reshape(-1).astype(jnp.int32)
    paged = (cache_k, cache_v, pt_flat, n_pages)

    xp = x_prompt.reshape(bp * tp, d_model)
    xs = x_sample.reshape(bs * ts, d_model)
    mem_rows = mem_prompt.reshape(bp * n_mem, d_model)
    outs = [[] for _ in range(8)]
    for i in range(depth):
        mem_kv = _norm_matmul(mem_rows, ln_mem[i], w_mem_kv_b[i]).reshape(bp, n_mem, 2 * mem_width)
        mk_p, mv_p = mem_kv[..., :mem_width], mem_kv[..., mem_width:]
        xp, kp, vp, cp = _layer(i, xp, pos_p, mk_p, mv_p, conv0, wts, batch=bp, seq=tp)
        xs, k_s, v_s, c_s = _layer(i, xs, pos_s, cache_mem_k[i], cache_mem_v[i], state_conv[i], wts,
                                   batch=bs, seq=ts, paged=paged)
        for lst, val in zip(outs, (kp.reshape(bp, tp, self_width), vp.reshape(bp, tp, self_width), mk_p, mv_p, cp,
                                   k_s.reshape(bs, ts, self_width), v_s.reshape(bs, ts, self_width), c_s)):
            lst.append(val)
    y_prompt = _rmsnorm(xp, ln_final).reshape(bp, tp, d_model)
    y_sample = _rmsnorm(xs, ln_final).reshape(bs, ts, d_model)
    return (y_prompt, y_sample) + tuple(jnp.stack(lst) for lst in outs)
```

```python
import functools
import math

import jax
import jax.numpy as jnp
from jax import lax
from jax.experimental import pallas as pl
from jax.experimental.pallas import tpu as pltpu

F32 = jnp.float32
BF16 = jnp.bfloat16

HEAD_DIM = 128
MOBA_BLOCK = 256
MOBA_TOPK = 3
CONV_W = 3
ROPE_THETA = 10000.0
EPS = 1e-6
SUBLN_EPS = 1e-5
N_MIXERS = 2

VMEM_LIMIT_BYTES = 54 * 1024 * 1024
NEG_INF = float("-inf")
BF16_SUBLANES = 16
PAGES_PER_STEP = 4
BLOCKS_PER_STEP = 4
QUERIES_PER_STEP = 4


def _params(*sem):
    return pltpu.CompilerParams(dimension_semantics=sem, vmem_limit_bytes=VMEM_LIMIT_BYTES)


def _pick(n, prefs):
    for p in prefs:
        if n % p == 0:
            return p
    return n


def _nt_dot(a, b):
    return lax.dot_general(a, b, (((1,), (1,)), ((), ())), preferred_element_type=F32)


def _pad_rows(x, rows):
    if x.shape[0] >= rows:
        return x
    return jnp.concatenate([x, jnp.zeros((rows - x.shape[0],) + x.shape[1:], x.dtype)], axis=0)


def _norm_matmul_kernel(x_ref, g_ref, w_ref, o_ref, h_ref):
    @pl.when(pl.program_id(1) == 0)
    def _():
        x = x_ref[...]
        ms = jnp.mean(x * x, axis=-1, keepdims=True)
        h_ref[...] = (x * lax.rsqrt(ms + EPS) * g_ref[...]).astype(BF16)

    o_ref[...] = jnp.dot(h_ref[...], w_ref[...].astype(BF16), preferred_element_type=F32)


def _norm_matmul(x, g, w_stack, layer):
    m, d = x.shape
    n = w_stack.shape[2]
    tm = _pick(m, (1024, 512, 256, 128, 64))
    tn = _pick(n, (512, 256, 128))
    return pl.pallas_call(
        _norm_matmul_kernel,
        grid=(m // tm, n // tn),
        in_specs=[
            pl.BlockSpec((tm, d), lambda i, j: (i, 0)),
            pl.BlockSpec((1, d), lambda i, j: (0, 0)),
            pl.BlockSpec((None, d, tn), lambda i, j: (layer, 0, j)),
        ],
        out_specs=pl.BlockSpec((tm, tn), lambda i, j: (i, j)),
        out_shape=jax.ShapeDtypeStruct((m, n), F32),
        scratch_shapes=[pltpu.VMEM((tm, d), BF16)],
        compiler_params=_params("parallel", "arbitrary"),
        name="norm_matmul",
    )(x, g.reshape(1, d), w_stack)


def _matmul_res_kernel(*refs, n_parts):
    a_refs = refs[:n_parts]
    w_refs = refs[n_parts:2 * n_parts]
    r_ref, o_ref = refs[2 * n_parts:]
    acc = r_ref[...]
    for a_ref, w_ref in zip(a_refs, w_refs):
        acc = acc + jnp.dot(a_ref[...], w_ref[...].astype(BF16), preferred_element_type=F32)
    o_ref[...] = acc


def _matmul_res(a_parts, w_stack, layer, res, *, tn_prefs=(512, 256, 128)):
    m, n = res.shape
    tm = _pick(m, (1024, 512, 256, 128, 64))
    tn = _pick(n, tn_prefs)
    a_specs, w_specs = [], []
    row0 = 0
    for a in a_parts:
        k = a.shape[1]
        assert row0 % k == 0
        a_specs.append(pl.BlockSpec((tm, k), lambda i, j: (i, 0)))
        w_specs.append(pl.BlockSpec((None, k, tn), lambda i, j, rb=row0 // k: (layer, rb, j)))
        row0 += k
    assert row0 == w_stack.shape[1]
    io_spec = pl.BlockSpec((tm, tn), lambda i, j: (i, j))
    return pl.pallas_call(
        functools.partial(_matmul_res_kernel, n_parts=len(a_parts)),
        grid=(m // tm, n // tn),
        in_specs=a_specs + w_specs + [io_spec],
        out_specs=io_spec,
        out_shape=jax.ShapeDtypeStruct((m, n), F32),
        compiler_params=_params("parallel", "parallel"),
        name="matmul_res",
    )(*a_parts, *([w_stack] * len(a_parts)), res)


def _rmsnorm_kernel(x_ref, g_ref, o_ref):
    x = x_ref[...]
    ms = jnp.mean(x * x, axis=-1, keepdims=True)
    o_ref[...] = x * lax.rsqrt(ms + EPS) * g_ref[...]


def _rmsnorm(x, g):
    m, d = x.shape
    tm = _pick(m, (512, 256, 128, 64))
    return pl.pallas_call(
        _rmsnorm_kernel,
        grid=(m // tm,),
        in_specs=[pl.BlockSpec((tm, d), lambda i: (i, 0)), pl.BlockSpec((1, d), lambda i: (0, 0))],
        out_specs=pl.BlockSpec((tm, d), lambda i: (i, 0)),
        out_shape=jax.ShapeDtypeStruct((m, d), F32),
        compiler_params=_params("parallel"),
        name="rmsnorm",
    )(x, g.reshape(1, d))


def _rope_tables(pos, head_dim):
    half = head_dim // 2
    inv = jnp.exp(-math.log(ROPE_THETA) * jnp.arange(half, dtype=F32) / half)
    ang = pos.astype(F32)[:, None] * inv[None, :]
    cos, sin = jnp.cos(ang), jnp.sin(ang)
    reps = HEAD_DIM // head_dim
    cos_t = jnp.tile(jnp.concatenate([cos, cos], axis=-1), (1, reps))
    sin_t = jnp.tile(jnp.concatenate([-sin, sin], axis=-1), (1, reps))
    return cos_t, sin_t


def _rope_kernel(*refs, head_dim, n_stack_in, with_qf, with_bf16, with_means):
    refs = refs[n_stack_in:]
    q_ref, k_ref, v_ref, cos_ref, sin_ref = refs[:5]
    outs = list(refs[5:])
    kf_ref, vf_ref = outs[:2]
    outs = outs[2:]
    qf_ref = outs.pop(0) if with_qf else None
    if with_bf16:
        qb_ref, kb_ref, vb_ref = outs[:3]
        outs = outs[3:]
    mean_ref = outs.pop(0) if with_means else None
    cos = cos_ref[...]
    sin = sin_ref[...]
    half = head_dim // 2
    n_groups = q_ref.shape[1] // HEAD_DIM
    if head_dim != HEAD_DIM:
        lane = lax.broadcasted_iota(jnp.int32, cos.shape, 1)
        first_half = (lane % head_dim) < half

    def rot(x):
        if head_dim == HEAD_DIM:
            partner = pltpu.roll(x, half, 1)
        else:
            partner = jnp.where(first_half, pltpu.roll(x, HEAD_DIM - half, 1), pltpu.roll(x, half, 1))
        return x * cos + partner * sin

    for h in range(n_groups):
        sl = slice(h * HEAD_DIM, (h + 1) * HEAD_DIM)
        qr = rot(q_ref[:, sl])
        kr = rot(k_ref[:, sl])
        kf_ref[:, sl] = kr
        if with_qf:
            qf_ref[:, sl] = qr
        if with_bf16:
            qb_ref[:, sl] = qr.astype(BF16)
            kb_ref[:, sl] = kr.astype(BF16)
        if with_means:
            mean_ref[:, sl] = jnp.sum(kr, axis=0, keepdims=True) * (1.0 / MOBA_BLOCK)
    v = v_ref[...]
    vf_ref[...] = v
    if with_bf16:
        vb_ref[...] = v.astype(BF16)


def _rope(proj, cos_t, sin_t, *, seq, width, head_dim, with_qf, with_bf16, with_means, stack=None):
    m = proj.shape[0]
    tr = MOBA_BLOCK if seq % MOBA_BLOCK == 0 else seq
    n_t = seq // tr
    if with_means:
        assert tr == MOBA_BLOCK
    row = lambda c: pl.BlockSpec((tr, width), lambda r: (r, c))
    tab = pl.BlockSpec((tr, HEAD_DIM), lambda r: (r % n_t, 0))
    in_specs = [row(0), row(1), row(2), tab, tab]
    args = [proj, proj, proj, cos_t, sin_t]
    aliases = {}
    if stack is None:
        names = ["k", "v"]
        out_shapes = [jax.ShapeDtypeStruct((m, width), F32)] * 2
        out_specs = [row(0)] * 2
        n_stack_in = 0
    else:
        layer, depth, k_stack, v_stack = stack
        names = ["k", "v"]
        out_shapes = [jax.ShapeDtypeStruct((depth, m, width), F32)] * 2
        out_specs = [pl.BlockSpec((None, tr, width), lambda r: (layer, r, 0))] * 2
        n_stack_in = 0
        if k_stack is not None:
            n_stack_in = 2
            in_specs = [pl.BlockSpec(memory_space=pl.ANY)] * 2 + in_specs
            args = [k_stack, v_stack] + args
            aliases = {0: 0, 1: 1}
    if with_qf:
        names.append("q")
        out_shapes.append(jax.ShapeDtypeStruct((m, width), F32))
        out_specs.append(row(0))
    if with_bf16:
        names += ["qb", "kb", "vb"]
        out_shapes += [jax.ShapeDtypeStruct((m, width), BF16)] * 3
        out_specs += [row(0)] * 3
    if with_means:
        names.append("means")
        out_shapes.append(jax.ShapeDtypeStruct((m // tr, 1, width), F32))
        out_specs.append(pl.BlockSpec((None, 1, width), lambda r: (r, 0, 0)))
    outs = pl.pallas_call(
        functools.partial(_rope_kernel, head_dim=head_dim, n_stack_in=n_stack_in, with_qf=with_qf,
                          with_bf16=with_bf16, with_means=with_means),
        grid=(m // tr,),
        in_specs=in_specs,
        out_specs=out_specs,
        out_shape=out_shapes,
        input_output_aliases=aliases,
        compiler_params=_params("parallel"),
        name="rope",
    )(*args)
    return dict(zip(names, outs))


def _softmax_pv(s_blocks, v_ref):
    blk = MOBA_BLOCK
    m = jnp.max(s_blocks[0], axis=1, keepdims=True)
    for s in s_blocks[1:]:
        m = jnp.maximum(m, jnp.max(s, axis=1, keepdims=True))
    l = None
    acc = None
    for kj, s in enumerate(s_blocks):
        p = jnp.exp(s - m)
        ls = jnp.sum(p, axis=1, keepdims=True)
        pv = jnp.dot(p.astype(BF16), v_ref[kj * blk:(kj + 1) * blk, :], preferred_element_type=F32)
        l = ls if l is None else l + ls
        acc = pv if acc is None else acc + pv
    return acc / l


def _moba_prompt_kernel(qb_ref, qf_ref, k_ref, v_ref, mean_ref, o_ref):
    blk = MOBA_BLOCK
    nq = qb_ref.shape[0] // blk
    scale = HEAD_DIM ** -0.5
    means = mean_ref[...]
    causal = lax.broadcasted_iota(jnp.int32, (blk, blk), 1) <= lax.broadcasted_iota(jnp.int32, (blk, blk), 0)
    for qi in range(nq):
        rows = slice(qi * blk, (qi + 1) * blk)
        q = qb_ref[rows, :]
        sel_bias = None
        if qi > MOBA_TOPK:
            gate = lax.dot_general(qf_ref[rows, :], means, (((1,), (1,)), ((), ())),
                                   precision=lax.Precision.HIGHEST, preferred_element_type=F32)
            blk_id = lax.broadcasted_iota(jnp.int32, gate.shape, 1)
            rank = jnp.zeros(gate.shape, F32)
            for m_id in range(qi):
                gm = gate[:, m_id:m_id + 1]
                beats = (gm > gate) | ((gm == gate) & (blk_id > m_id))
                rank = rank + jnp.where(beats, 1.0, 0.0)
            sel_bias = jnp.where(rank < float(MOBA_TOPK), 0.0, NEG_INF)
        s_blocks = []
        for kj in range(qi + 1):
            s = _nt_dot(q, k_ref[kj * blk:(kj + 1) * blk, :]) * scale
            if kj == qi:
                s = jnp.where(causal, s, NEG_INF)
            elif sel_bias is not None:
                s = s + sel_bias[:, kj:kj + 1]
            s_blocks.append(s)
        o_ref[rows, :] = _softmax_pv(s_blocks, v_ref).astype(o_ref.dtype)


def _moba_prompt(qb, qf, kb, vb, means, *, batch, seq):
    m, width = qb.shape
    n_heads = width // HEAD_DIM
    nb = means.shape[1]
    spec = pl.BlockSpec((seq, HEAD_DIM), lambda b, h: (b, h))
    return pl.pallas_call(
        _moba_prompt_kernel,
        grid=(batch, n_heads),
        in_specs=[spec, spec, spec, spec, pl.BlockSpec((None, nb, HEAD_DIM), lambda b, h: (b, 0, h))],
        out_specs=spec,
        out_shape=jax.ShapeDtypeStruct((m, width), BF16),
        compiler_params=_params("parallel", "parallel"),
        name="moba_prompt",
    )(qb, qf, kb, vb, means)


def _diff_lambda(lp_ref):
    lp = lp_ref[...]
    s1 = jnp.sum(lp[0:1, :] * lp[1:2, :], axis=1, keepdims=True)
    s2 = jnp.sum(lp[2:3, :] * lp[3:4, :], axis=1, keepdims=True)
    return jnp.exp(s1) - jnp.exp(s2)


def _diff_finish(o1, o2, lam, subln, lam_init):
    o = o1 - lam * o2
    ms = jnp.mean(o * o, axis=-1, keepdims=True)
    return (o * lax.rsqrt(ms + SUBLN_EPS) * subln) * (1.0 - lam_init)


def _diff_prompt_kernel(qb_ref, k_ref, v_ref, lp_ref, subln_ref, o_ref, *, lam_init):
    blk = MOBA_BLOCK
    nq = qb_ref.shape[0] // blk
    dd = HEAD_DIM // 2
    scale = dd ** -0.5
    lam = _diff_lambda(lp_ref) + lam_init
    subln = subln_ref[...]
    lane = lax.broadcasted_iota(jnp.int32, (blk, HEAD_DIM), 1)
    r_id = lax.broadcasted_iota(jnp.int32, (2 * blk, blk), 0) % blk
    causal = lax.broadcasted_iota(jnp.int32, (2 * blk, blk), 1) <= r_id
    for qi in range(nq):
        rows = slice(qi * blk, (qi + 1) * blk)
        q = qb_ref[rows, :].astype(F32)
        q_stack = jnp.concatenate([jnp.where(lane < dd, q, 0.0), jnp.where(lane >= dd, q, 0.0)], axis=0).astype(BF16)
        s_blocks = []
        for kj in range(qi + 1):
            s = _nt_dot(q_stack, k_ref[kj * blk:(kj + 1) * blk, :]) * scale
            if kj == qi:
                s = jnp.where(causal, s, NEG_INF)
            s_blocks.append(s)
        o = _softmax_pv(s_blocks, v_ref)
        o_ref[rows, :] = _diff_finish(o[:blk], o[blk:], lam, subln, lam_init).astype(o_ref.dtype)


def _diff_prompt(qb, kb, vb, lp, subln, *, batch, seq, lam_init):
    m, width = qb.shape
    n_heads = width // HEAD_DIM
    spec = pl.BlockSpec((seq, HEAD_DIM), lambda b, h: (b, h))
    return pl.pallas_call(
        functools.partial(_diff_prompt_kernel, lam_init=lam_init),
        grid=(batch, n_heads),
        in_specs=[spec, spec, spec,
                  pl.BlockSpec(lp.shape, lambda b, h: (0, 0)),
                  pl.BlockSpec((1, HEAD_DIM), lambda b, h: (0, 0))],
        out_specs=spec,
        out_shape=jax.ShapeDtypeStruct((m, width), BF16),
        compiler_params=_params("parallel", "parallel"),
        name="diff_prompt",
    )(qb, kb, vb, lp, subln.reshape(1, HEAD_DIM))


def _mem_attn_kernel(q_ref, mk_ref, mv_ref, o_ref):
    tq = q_ref.shape[0]
    q = _pad_rows(q_ref[...], BF16_SUBLANES).astype(BF16)
    s = _nt_dot(q, mk_ref[...].astype(BF16)) * (HEAD_DIM ** -0.5)
    m = jnp.max(s, axis=1, keepdims=True)
    p = jnp.exp(s - m)
    l = jnp.sum(p, axis=1, keepdims=True)
    o = jnp.dot(p.astype(BF16), mv_ref[...].astype(BF16), preferred_element_type=F32)
    o_ref[...] = (o / l)[:tq].astype(o_ref.dtype)


def _mem_attn(proj, mem_k, mem_v, *, batch, seq, q_col_block, kv_col_blocks):
    m = proj.shape[0]
    n_mem = mem_k.shape[1]
    n_h = (proj.shape[1] // HEAD_DIM) - q_col_block
    tq = _pick(seq, (512, 256, 128, 64, 32, 16, 8))
    nq = seq // tq
    k_spec = pl.BlockSpec((None, n_mem, HEAD_DIM), lambda b, i, h: (b, 0, kv_col_blocks[0] + h))
    v_spec = pl.BlockSpec((None, n_mem, HEAD_DIM), lambda b, i, h: (b, 0, kv_col_blocks[1] + h))
    return pl.pallas_call(
        _mem_attn_kernel,
        grid=(batch, nq, n_h),
        in_specs=[pl.BlockSpec((tq, HEAD_DIM), lambda b, i, h: (b * nq + i, q_col_block + h)), k_spec, v_spec],
        out_specs=pl.BlockSpec((tq, HEAD_DIM), lambda b, i, h: (b * nq + i, h)),
        out_shape=jax.ShapeDtypeStruct((m, n_h * HEAD_DIM), BF16 if tq % BF16_SUBLANES == 0 else F32),
        compiler_params=_params("parallel", "parallel", "parallel"),
        name="mem_attn",
    )(proj, mem_k, mem_v)


def _conv_gate_kernel(g_ref, a_ref, st_ref, w_ref, b_ref, act_ref, nst_ref):
    g = g_ref[...]
    seq = g.shape[0]
    st = st_ref[...]
    w = w_ref[...]
    row = lax.broadcasted_iota(jnp.int32, g.shape, 0)
    g1 = jnp.where(row == 0, st[1:2, :], pltpu.roll(g, 1, 0))
    g2 = jnp.where(row == 0, st[0:1, :], jnp.where(row == 1, st[1:2, :], pltpu.roll(g, 2, 0)))
    c = b_ref[...] + w[0:1, :] * g2 + w[1:2, :] * g1 + w[2:3, :] * g
    act_ref[...] = (c / (1.0 + jnp.exp(-c)) * a_ref[...]).astype(act_ref.dtype)
    nst_ref[...] = g[seq - (CONV_W - 1):, :]


def _conv_gate(u, state, conv_w, conv_b, *, batch, seq):
    d_ff = u.shape[1] // 2
    assert seq >= CONV_W - 1
    tc = _pick(d_ff, (512, 256, 128)) if seq > 64 else d_ff
    nc = d_ff // tc
    return pl.pallas_call(
        _conv_gate_kernel,
        grid=(batch, nc),
        in_specs=[
            pl.BlockSpec((seq, tc), lambda b, j: (b, j)),
            pl.BlockSpec((seq, tc), lambda b, j: (b, nc + j)),
            pl.BlockSpec((None, CONV_W - 1, tc), lambda b, j: (b, 0, j)),
            pl.BlockSpec((CONV_W, tc), lambda b, j: (0, j)),
            pl.BlockSpec((1, tc), lambda b, j: (0, j)),
        ],
        out_specs=[
            pl.BlockSpec((seq, tc), lambda b, j: (b, j)),
            pl.BlockSpec((None, CONV_W - 1, tc), lambda b, j: (b, 0, j)),
        ],
        out_shape=[
            jax.ShapeDtypeStruct((batch * seq, d_ff), BF16),
            jax.ShapeDtypeStruct((batch, CONV_W - 1, d_ff), F32),
        ],
        compiler_params=_params("parallel", "parallel"),
        name="conv_gate",
    )(u, u, state, conv_w, conv_b.reshape(1, d_ff))


def _page_mean_kernel(pt_ref, *refs):
    del pt_ref
    o_ref = refs[-1]
    for g in range(o_ref.shape[0]):
        s = jnp.sum(refs[2 * g][...], axis=0, keepdims=True) + jnp.sum(refs[2 * g + 1][...], axis=0, keepdims=True)
        o_ref[g:g + 1, :] = s * (1.0 / MOBA_BLOCK)


def _page_means(cache_k, page_table_flat, layer, *, batch, n_pages):
    page, width = cache_k.shape[2:]
    assert MOBA_BLOCK == 2 * page
    n_blocks = n_pages // 2
    nbs = _pick(n_blocks, (BLOCKS_PER_STEP, 2, 1))
    n_steps = n_blocks // nbs

    def k_spec(j):
        return pl.BlockSpec((None, None, page, width),
                            lambda b, n, pt: (layer, pt[b * n_pages + 2 * nbs * n + j], 0, 0))

    out = pl.pallas_call(
        _page_mean_kernel,
        grid_spec=pltpu.PrefetchScalarGridSpec(
            num_scalar_prefetch=1,
            grid=(batch, n_steps),
            in_specs=[k_spec(j) for j in range(2 * nbs)],
            out_specs=pl.BlockSpec((None, None, nbs, width), lambda b, n, pt: (b, n, 0, 0)),
        ),
        out_shape=jax.ShapeDtypeStruct((batch, n_steps, nbs, width), F32),
        compiler_params=_params("parallel", "parallel"),
        name="page_means",
    )(page_table_flat, *([cache_k] * (2 * nbs)))
    return out.reshape(batch, n_blocks, width)


def _topk_kernel(q_ref, mean_ref, o_ref):
    n_heads = q_ref.shape[1] // HEAD_DIM
    n_blocks = mean_ref.shape[0]
    for h in range(n_heads):
        sl = slice(h * HEAD_DIM, (h + 1) * HEAD_DIM)
        gate = lax.dot_general(q_ref[:, sl], mean_ref[:, sl], (((1,), (1,)), ((), ())),
                               precision=lax.Precision.HIGHEST, preferred_element_type=F32)
        lane = lax.broadcasted_iota(jnp.int32, gate.shape, 1)
        out_lane = lax.broadcasted_iota(jnp.int32, (gate.shape[0], HEAD_DIM), 1)
        out = jnp.zeros((gate.shape[0], HEAD_DIM), jnp.int32)
        for r in range(MOBA_TOPK):
            mx = jnp.max(gate, axis=1, keepdims=True)
            idx = jnp.min(jnp.where(gate == mx, lane, n_blocks), axis=1, keepdims=True)
            out = jnp.where(out_lane == r, idx, out)
            gate = jnp.where(lane == idx, NEG_INF, gate)
        o_ref[h] = out


def _sample_topk(qf, means, *, batch, seq):
    width = qf.shape[1]
    n_heads = width // HEAD_DIM
    n_blocks = means.shape[1]
    return pl.pallas_call(
        _topk_kernel,
        grid=(batch,),
        in_specs=[pl.BlockSpec((seq, width), lambda b: (b, 0)),
                  pl.BlockSpec((None, n_blocks, width), lambda b: (b, 0, 0))],
        out_specs=pl.BlockSpec((None, n_heads, seq, HEAD_DIM), lambda b: (b, 0, 0, 0)),
        out_shape=jax.ShapeDtypeStruct((batch, n_heads, seq, HEAD_DIM), jnp.int32),
        compiler_params=_params("parallel"),
        name="sample_topk",
    )(qf, means)


def _moba_sample_kernel(pt_ref, idx_ref, q_ref, kn_ref, vn_ref, *refs, nqs):
    del pt_ref, idx_ref
    n_tiles = 2 * MOBA_TOPK
    k_refs = refs[:nqs * n_tiles]
    v_refs = refs[nqs * n_tiles:2 * nqs * n_tiles]
    o_ref = refs[2 * nqs * n_tiles]
    c0 = pl.program_id(2) * nqs
    scale = HEAD_DIM ** -0.5
    seq = q_ref.shape[0]
    k_own = kn_ref[...]
    v_own = vn_ref[...]
    key_id = lax.broadcasted_iota(jnp.int32, (seq, 1), 0)

    @pl.when(c0 == 0)
    def _():
        o_ref[...] = jnp.zeros_like(o_ref)

    out = o_ref[...]
    row = lax.broadcasted_iota(jnp.int32, out.shape, 0)
    for j in range(nqs):
        c = c0 + j
        q_row = q_ref[pl.ds(c, 1), :]
        s_own = jnp.sum(k_own * q_row, axis=1, keepdims=True) * scale
        s_own = jnp.where(key_id <= c, s_own, NEG_INF)
        s_past = [jnp.sum(k_ref[...] * q_row, axis=1, keepdims=True) * scale
                  for k_ref in k_refs[j * n_tiles:(j + 1) * n_tiles]]
        m = jnp.max(s_own, axis=0, keepdims=True)
        for s in s_past:
            m = jnp.maximum(m, jnp.max(s, axis=0, keepdims=True))
        p_own = jnp.exp(s_own - m)
        l = jnp.sum(p_own, axis=0, keepdims=True)
        acc = jnp.sum(p_own * v_own, axis=0, keepdims=True)
        for s, v_ref in zip(s_past, v_refs[j * n_tiles:(j + 1) * n_tiles]):
            p = jnp.exp(s - m)
            l = l + jnp.sum(p, axis=0, keepdims=True)
            acc = acc + jnp.sum(p * v_ref[...], axis=0, keepdims=True)
        out = jnp.where(row == c, acc / l, out)
    o_ref[...] = out


def _moba_sample(qf, k_new, v_new, cache_k, cache_v, page_table_flat, idx_flat, layer, *, batch, seq, n_pages):
    width = qf.shape[1]
    n_heads = width // HEAD_DIM
    page = cache_k.shape[2]
    assert n_pages // 2 >= MOBA_TOPK
    nqs = _pick(seq, (QUERIES_PER_STEP, 2, 1))

    def kv_spec(j, slot, half):
        def imap(b, h, c, pt, idx):
            blk = idx[((b * n_heads + h) * seq + c * nqs + j) * MOBA_TOPK + slot]
            return (layer, pt[b * n_pages + 2 * blk + half], 0, h)
        return pl.BlockSpec((None, None, page, HEAD_DIM), imap)

    tiles = [kv_spec(j, s, half) for j in range(nqs) for s in range(MOBA_TOPK) for half in range(2)]
    new_spec = pl.BlockSpec((seq, HEAD_DIM), lambda b, h, c, pt, idx: (b, h))
    return pl.pallas_call(
        functools.partial(_moba_sample_kernel, nqs=nqs),
        grid_spec=pltpu.PrefetchScalarGridSpec(
            num_scalar_prefetch=2,
            grid=(batch, n_heads, seq // nqs),
            in_specs=[new_spec, new_spec, new_spec] + tiles + tiles,
            out_specs=new_spec,
        ),
        out_shape=jax.ShapeDtypeStruct((batch * seq, width), F32),
        compiler_params=_params("parallel", "parallel", "arbitrary"),
        name="moba_sample",
    )(page_table_flat, idx_flat, qf, k_new, v_new, *([cache_k] * len(tiles)), *([cache_v] * len(tiles)))


def _diff_sample_kernel(pt_ref, q_ref, kn_ref, vn_ref, *refs, n_pg, lam_init):
    del pt_ref
    k_refs = refs[:n_pg]
    v_refs = refs[n_pg:2 * n_pg]
    lp_ref, subln_ref, o_ref, qbd_ref, m_ref, l_ref, acc_ref = refs[2 * n_pg:]
    p_id = pl.program_id(1)
    n_p = pl.num_programs(1)
    seq, width = q_ref.shape
    page = k_refs[0].shape[0]
    dd = HEAD_DIM // 2
    n_sub = width // dd
    n_heads = width // HEAD_DIM
    scale = dd ** -0.5
    g = 2 * seq

    @pl.when(p_id == 0)
    def _():
        q = q_ref[...]
        q_rep = jnp.concatenate([q] * n_sub, axis=0)
        r_sub = lax.broadcasted_iota(jnp.int32, q_rep.shape, 0) // seq
        c_sub = lax.broadcasted_iota(jnp.int32, q_rep.shape, 1) // dd
        qbd_ref[...] = jnp.where(r_sub == c_sub, q_rep, 0.0).astype(BF16)
        m_ref[...] = jnp.full(m_ref.shape, NEG_INF, F32)
        l_ref[...] = jnp.zeros(l_ref.shape, F32)
        acc_ref[...] = jnp.zeros(acc_ref.shape, F32)

    def update(s_list, v_list):
        m_old = m_ref[...]
        m_new = m_old
        for s in s_list:
            m_new = jnp.maximum(m_new, jnp.max(s, axis=1, keepdims=True))
        alpha = jnp.exp(m_old - m_new)
        p_list = [jnp.exp(s - m_new).astype(BF16) for s in s_list]
        l_new = alpha * l_ref[...]
        for p in p_list:
            l_new = l_new + jnp.sum(p.astype(F32), axis=1, keepdims=True)
        l_ref[...] = l_new
        m_ref[...] = m_new
        for h in range(n_heads):
            hr = slice(h * g, (h + 1) * g)
            hc = slice(h * HEAD_DIM, (h + 1) * HEAD_DIM)
            acc = alpha[hr, :] * acc_ref[hr, :]
            for p, v in zip(p_list, v_list):
                acc = acc + jnp.dot(p[hr, :], v[:, hc], preferred_element_type=F32)
            acc_ref[hr, :] = acc

    qbd = qbd_ref[...]
    update([_nt_dot(qbd, k_ref[...].astype(BF16)) * scale for k_ref in k_refs],
           [v_ref[...].astype(BF16) for v_ref in v_refs])

    @pl.when(p_id == n_p - 1)
    def _():
        s = _nt_dot(qbd_ref[...], _pad_rows(kn_ref[...], page).astype(BF16)) * scale
        q_pos = lax.broadcasted_iota(jnp.int32, s.shape, 0) % seq
        k_pos = lax.broadcasted_iota(jnp.int32, s.shape, 1)
        update([jnp.where(k_pos <= q_pos, s, NEG_INF)], [_pad_rows(vn_ref[...], page).astype(BF16)])
        o = acc_ref[...] / l_ref[...]
        lam = _diff_lambda(lp_ref) + lam_init
        subln = subln_ref[...]
        for h in range(n_heads):
            o1 = o[h * g:h * g + seq, :]
            o2 = o[h * g + seq:(h + 1) * g, :]
            o_ref[:, h * HEAD_DIM:(h + 1) * HEAD_DIM] = _diff_finish(o1, o2, lam, subln, lam_init)


def _diff_sample(qf, k_new, v_new, cache_k, cache_v, page_table_flat, lp, subln, layer, *,
                 batch, seq, n_pages, lam_init):
    width = qf.shape[1]
    page = cache_k.shape[2]
    rows = (width // (HEAD_DIM // 2)) * seq
    n_pg = _pick(n_pages, (PAGES_PER_STEP, 2, 1))
    new_spec = pl.BlockSpec((seq, width), lambda b, p, pt: (b, 0))

    def kv_spec(j):
        return pl.BlockSpec((None, None, page, width),
                            lambda b, p, pt: (layer, pt[b * n_pages + n_pg * p + j], 0, 0))

    kv_specs = [kv_spec(j) for j in range(n_pg)]
    return pl.pallas_call(
        functools.partial(_diff_sample_kernel, n_pg=n_pg, lam_init=lam_init),
        grid_spec=pltpu.PrefetchScalarGridSpec(
            num_scalar_prefetch=1,
            grid=(batch, n_pages // n_pg),
            in_specs=[new_spec, new_spec, new_spec] + kv_specs + kv_specs +
                     [pl.BlockSpec(lp.shape, lambda b, p, pt: (0, 0)),
                      pl.BlockSpec((1, HEAD_DIM), lambda b, p, pt: (0, 0))],
            out_specs=new_spec,
            scratch_shapes=[pltpu.VMEM((rows, width), BF16), pltpu.VMEM((rows, 1), F32),
                            pltpu.VMEM((rows, 1), F32), pltpu.VMEM((rows, HEAD_DIM), F32)],
        ),
        out_shape=jax.ShapeDtypeStruct((batch * seq, width), F32),
        compiler_params=_params("parallel", "arbitrary"),
        name="diff_sample",
    )(page_table_flat, qf, k_new, v_new, *([cache_k] * n_pg), *([cache_v] * n_pg), lp, subln.reshape(1, HEAD_DIM))


def _layer(i, x, pos, mem_k, mem_v, mem_cols, conv_state, wts, *, batch, seq, paged=None, stack=None):
    width = wts["self_width"]
    is_moba = i % N_MIXERS == 0
    prompt = paged is None
    head_dim = HEAD_DIM if is_moba else HEAD_DIM // 2
    proj = _norm_matmul(x, wts["ln_attn"][i], wts["w_in"], i)
    cos_t, sin_t = _rope_tables(pos, head_dim)
    r = _rope(proj, cos_t, sin_t, seq=seq, width=width, head_dim=head_dim, with_qf=is_moba or not prompt,
              with_bf16=prompt, with_means=is_moba and prompt, stack=stack)
    if is_moba:
        if prompt:
            means = r["means"].reshape(batch, seq // MOBA_BLOCK, width)
            self_out = _moba_prompt(r["qb"], r["q"], r["kb"], r["vb"], means, batch=batch, seq=seq)
        else:
            cache_k, cache_v, pt_flat, n_pages = paged
            means = _page_means(cache_k, pt_flat, i, batch=batch, n_pages=n_pages)
            idx = _sample_topk(r["q"], means, batch=batch, seq=seq)
            self_out = _moba_sample(r["q"], r["k"], r["v"], cache_k, cache_v, pt_flat,
                                    idx[..., :MOBA_TOPK].reshape(-1), i,
                                    batch=batch, seq=seq, n_pages=n_pages).astype(BF16)
    else:
        j = i // N_MIXERS
        lam_init = 0.8 - 0.6 * math.exp(-0.3 * i)
        lp = wts["diff_lambda"][j]
        subln = wts["diff_subln"][j]
        if prompt:
            self_out = _diff_prompt(r["qb"], r["kb"], r["vb"], lp, subln, batch=batch, seq=seq, lam_init=lam_init)
        else:
            cache_k, cache_v, pt_flat, n_pages = paged
            self_out = _diff_sample(r["q"], r["k"], r["v"], cache_k, cache_v, pt_flat, lp, subln, i,
                                    batch=batch, seq=seq, n_pages=n_pages, lam_init=lam_init).astype(BF16)
    mem_out = _mem_attn(proj, mem_k, mem_v, batch=batch, seq=seq, q_col_block=3 * width // HEAD_DIM,
                        kv_col_blocks=mem_cols)
    x = _matmul_res([self_out, mem_out.astype(BF16)], wts["w_o"], i, x)
    u = _norm_matmul(x, wts["ln_ffn"][i], wts["w_up"], i)
    act, new_conv = _conv_gate(u, conv_state, wts["conv_w"][i], wts["conv_b"][i], batch=batch, seq=seq)
    x = _matmul_res([act], wts["w_down"], i, x, tn_prefs=(256, 128))
    return x, r["k"], r["v"], new_conv


def kernel(x_prompt, x_sample, mem_prompt, cache_k, cache_v, cache_mem_k, cache_mem_v, state_conv, page_table,
           ln_attn, w_in, w_o, diff_lambda, diff_subln, ln_mem, w_mem_kv, ln_ffn, w_up, conv_w, conv_b,
           w_down, ln_final):
    bp, tp, d_model = x_prompt.shape
    bs, ts, _ = x_sample.shape
    depth = w_in.shape[0]
    n_mem = mem_prompt.shape[1]
    mem_width = w_mem_kv.shape[2] // 2
    self_width = d_model - mem_width
    d_ff = w_down.shape[1]
    n_pages = page_table.shape[1]
    page = cache_k.shape[2]
    past = n_pages * page
    assert tp % MOBA_BLOCK == 0 and past % MOBA_BLOCK == 0 and ts <= MOBA_BLOCK

    wts = dict(self_width=self_width, ln_attn=ln_attn, ln_ffn=ln_ffn, diff_lambda=diff_lambda,
               diff_subln=diff_subln, conv_w=conv_w, conv_b=conv_b, w_in=w_in, w_o=w_o, w_up=w_up, w_down=w_down)
    pos_p = jnp.arange(tp, dtype=jnp.int32)
    pos_s = past + jnp.arange(ts, dtype=jnp.int32)
    conv0 = jnp.zeros((bp, CONV_W - 1, d_ff), F32)
    pt_flat = page_table.reshape(-1).astype(jnp.int32)
    paged = (cache_k, cache_v, pt_flat, n_pages)
    mem_heads = mem_width // HEAD_DIM

    xp = x_prompt.reshape(bp * tp, d_model)
    xs = x_sample.reshape(bs * ts, d_model)
    mem_rows = mem_prompt.reshape(bp * n_mem, d_model)
    k_stack = v_stack = None
    mkv_l, cp_l, ks_l, vs_l, cs_l = [], [], [], [], []
    for i in range(depth):
        mem_kv = _norm_matmul(mem_rows, ln_mem[i], w_mem_kv, i).reshape(bp, n_mem, 2 * mem_width)
        xp, k_stack, v_stack, cp = _layer(i, xp, pos_p, mem_kv, mem_kv, (0, mem_heads), conv0, wts,
                                          batch=bp, seq=tp, stack=(i, depth, k_stack, v_stack))
        xs, k_s, v_s, c_s = _layer(i, xs, pos_s, cache_mem_k[i], cache_mem_v[i], (0, 0), state_conv[i], wts,
                                   batch=bs, seq=ts, paged=paged)
        mkv_l.append(mem_kv)
        cp_l.append(cp)
        ks_l.append(k_s.reshape(bs, ts, self_width))
        vs_l.append(v_s.reshape(bs, ts, self_width))
        cs_l.append(c_s)
    y_prompt = _rmsnorm(xp, ln_final).reshape(bp, tp, d_model)
    y_sample = _rmsnorm(xs, ln_final).reshape(bs, ts, d_model)
    mkv = jnp.stack(mkv_l)
    return (y_prompt, y_sample,
            k_stack.reshape(depth, bp, tp, self_width), v_stack.reshape(depth, bp, tp, self_width),
            mkv[..., :mem_width], mkv[..., mem_width:], jnp.stack(cp_l),
            jnp.stack(ks_l), jnp.stack(vs_l), jnp.stack(cs_l))
```

```python
import functools
import math

import jax
import jax.numpy as jnp
from jax import lax
from jax.experimental import pallas as pl
from jax.experimental.pallas import tpu as pltpu

F32 = jnp.float32
BF16 = jnp.bfloat16

HEAD_DIM = 128
MOBA_BLOCK = 256
MOBA_TOPK = 3
CONV_W = 3
ROPE_THETA = 10000.0
EPS = 1e-6
SUBLN_EPS = 1e-5
N_MIXERS = 2

VMEM_LIMIT_BYTES = 54 * 1024 * 1024
NEG_INF = float("-inf")
BF16_SUBLANES = 16
FFN_ROW_TILE = 2048
FFN_ROW_CHUNK = 512
PAGES_PER_STEP = 8
BLOCKS_PER_STEP = 4
QUERIES_PER_STEP = 4


def _params(*sem):
    return pltpu.CompilerParams(dimension_semantics=sem, vmem_limit_bytes=VMEM_LIMIT_BYTES)


def _pick(n, prefs):
    for p in prefs:
        if n % p == 0:
            return p
    return n


def _nt_dot(a, b):
    return lax.dot_general(a, b, (((1,), (1,)), ((), ())), preferred_element_type=F32)


def _pad_rows(x, rows):
    if x.shape[0] >= rows:
        return x
    return jnp.concatenate([x, jnp.zeros((rows - x.shape[0],) + x.shape[1:], x.dtype)], axis=0)


def _norm_matmul_kernel(x_ref, g_ref, w_ref, o_ref, h_ref):
    @pl.when(pl.program_id(1) == 0)
    def _():
        x = x_ref[...]
        ms = jnp.mean(x * x, axis=-1, keepdims=True)
        h_ref[...] = (x * lax.rsqrt(ms + EPS) * g_ref[...]).astype(BF16)

    o_ref[...] = jnp.dot(h_ref[...], w_ref[...].astype(BF16), preferred_element_type=F32)


def _norm_matmul(x, g, w_stack, layer):
    m, d = x.shape
    n = w_stack.shape[2]
    tm = _pick(m, (1024, 512, 256, 128, 64))
    tn = _pick(n, (512, 256, 128))
    return pl.pallas_call(
        _norm_matmul_kernel,
        grid=(m // tm, n // tn),
        in_specs=[
            pl.BlockSpec((tm, d), lambda i, j: (i, 0)),
            pl.BlockSpec((1, d), lambda i, j: (0, 0)),
            pl.BlockSpec((None, d, tn), lambda i, j: (layer, 0, j)),
        ],
        out_specs=pl.BlockSpec((tm, tn), lambda i, j: (i, j)),
        out_shape=jax.ShapeDtypeStruct((m, n), F32),
        scratch_shapes=[pltpu.VMEM((tm, d), BF16)],
        compiler_params=_params("parallel", "arbitrary"),
        name="norm_matmul",
    )(x, g.reshape(1, d), w_stack)


def _matmul_res_kernel(*refs, n_parts):
    a_refs = refs[:n_parts]
    w_refs = refs[n_parts:2 * n_parts]
    r_ref, o_ref = refs[2 * n_parts:]
    acc = r_ref[...]
    for a_ref, w_ref in zip(a_refs, w_refs):
        acc = acc + jnp.dot(a_ref[...], w_ref[...].astype(BF16), preferred_element_type=F32)
    o_ref[...] = acc


def _matmul_res(a_parts, w_stack, layer, res, *, tn_prefs=(512, 256, 128)):
    m, n = res.shape
    tm = _pick(m, (1024, 512, 256, 128, 64))
    tn = _pick(n, tn_prefs)
    a_specs, w_specs = [], []
    row0 = 0
    for a in a_parts:
        k = a.shape[1]
        assert row0 % k == 0
        a_specs.append(pl.BlockSpec((tm, k), lambda i, j: (i, 0)))
        w_specs.append(pl.BlockSpec((None, k, tn), lambda i, j, rb=row0 // k: (layer, rb, j)))
        row0 += k
    assert row0 == w_stack.shape[1]
    io_spec = pl.BlockSpec((tm, tn), lambda i, j: (i, j))
    return pl.pallas_call(
        functools.partial(_matmul_res_kernel, n_parts=len(a_parts)),
        grid=(m // tm, n // tn),
        in_specs=a_specs + w_specs + [io_spec],
        out_specs=io_spec,
        out_shape=jax.ShapeDtypeStruct((m, n), F32),
        compiler_params=_params("parallel", "parallel"),
        name="matmul_res",
    )(*a_parts, *([w_stack] * len(a_parts)), res)


def _rmsnorm_kernel(x_ref, g_ref, o_ref):
    x = x_ref[...]
    ms = jnp.mean(x * x, axis=-1, keepdims=True)
    o_ref[...] = x * lax.rsqrt(ms + EPS) * g_ref[...]


def _rmsnorm(x, g):
    m, d = x.shape
    tm = _pick(m, (512, 256, 128, 64))
    return pl.pallas_call(
        _rmsnorm_kernel,
        grid=(m // tm,),
        in_specs=[pl.BlockSpec((tm, d), lambda i: (i, 0)), pl.BlockSpec((1, d), lambda i: (0, 0))],
        out_specs=pl.BlockSpec((tm, d), lambda i: (i, 0)),
        out_shape=jax.ShapeDtypeStruct((m, d), F32),
        compiler_params=_params("parallel"),
        name="rmsnorm",
    )(x, g.reshape(1, d))


def _rope_tables(pos, head_dim):
    half = head_dim // 2
    inv = jnp.exp(-math.log(ROPE_THETA) * jnp.arange(half, dtype=F32) / half)
    ang = pos.astype(F32)[:, None] * inv[None, :]
    cos, sin = jnp.cos(ang), jnp.sin(ang)
    reps = HEAD_DIM // head_dim
    cos_t = jnp.tile(jnp.concatenate([cos, cos], axis=-1), (1, reps))
    sin_t = jnp.tile(jnp.concatenate([-sin, sin], axis=-1), (1, reps))
    return cos_t, sin_t


def _rope_kernel(*refs, head_dim, n_stack_in, with_qf, with_bf16, with_means):
    refs = refs[n_stack_in:]
    q_ref, k_ref, v_ref, cos_ref, sin_ref = refs[:5]
    outs = list(refs[5:])
    kf_ref, vf_ref = outs[:2]
    outs = outs[2:]
    qf_ref = outs.pop(0) if with_qf else None
    if with_bf16:
        qb_ref, kb_ref, vb_ref = outs[:3]
        outs = outs[3:]
    mean_ref = outs.pop(0) if with_means else None
    cos = cos_ref[...]
    sin = sin_ref[...]
    half = head_dim // 2
    n_groups = q_ref.shape[1] // HEAD_DIM
    if head_dim != HEAD_DIM:
        lane = lax.broadcasted_iota(jnp.int32, cos.shape, 1)
        first_half = (lane % head_dim) < half

    def rot(x):
        if head_dim == HEAD_DIM:
            partner = pltpu.roll(x, half, 1)
        else:
            partner = jnp.where(first_half, pltpu.roll(x, HEAD_DIM - half, 1), pltpu.roll(x, half, 1))
        return x * cos + partner * sin

    for h in range(n_groups):
        sl = slice(h * HEAD_DIM, (h + 1) * HEAD_DIM)
        qr = rot(q_ref[:, sl])
        kr = rot(k_ref[:, sl])
        kf_ref[:, sl] = kr
        if with_qf:
            qf_ref[:, sl] = qr
        if with_bf16:
            qb_ref[:, sl] = qr.astype(BF16)
            kb_ref[:, sl] = kr.astype(BF16)
        if with_means:
            mean_ref[:, sl] = jnp.sum(kr, axis=0, keepdims=True) * (1.0 / MOBA_BLOCK)
    v = v_ref[...]
    vf_ref[...] = v
    if with_bf16:
        vb_ref[...] = v.astype(BF16)


def _rope(proj, cos_t, sin_t, *, seq, width, head_dim, with_qf, with_bf16, with_means, stack=None):
    m = proj.shape[0]
    tr = MOBA_BLOCK if seq % MOBA_BLOCK == 0 else seq
    n_t = seq // tr
    if with_means:
        assert tr == MOBA_BLOCK
    row = lambda c: pl.BlockSpec((tr, width), lambda r: (r, c))
    tab = pl.BlockSpec((tr, HEAD_DIM), lambda r: (r % n_t, 0))
    in_specs = [row(0), row(1), row(2), tab, tab]
    args = [proj, proj, proj, cos_t, sin_t]
    aliases = {}
    if stack is None:
        names = ["k", "v"]
        out_shapes = [jax.ShapeDtypeStruct((m, width), F32)] * 2
        out_specs = [row(0)] * 2
        n_stack_in = 0
    else:
        layer, depth, k_stack, v_stack = stack
        names = ["k", "v"]
        out_shapes = [jax.ShapeDtypeStruct((depth, m, width), F32)] * 2
        out_specs = [pl.BlockSpec((None, tr, width), lambda r: (layer, r, 0))] * 2
        n_stack_in = 0
        if k_stack is not None:
            n_stack_in = 2
            in_specs = [pl.BlockSpec(memory_space=pl.ANY)] * 2 + in_specs
            args = [k_stack, v_stack] + args
            aliases = {0: 0, 1: 1}
    if with_qf:
        names.append("q")
        out_shapes.append(jax.ShapeDtypeStruct((m, width), F32))
        out_specs.append(row(0))
    if with_bf16:
        names += ["qb", "kb", "vb"]
        out_shapes += [jax.ShapeDtypeStruct((m, width), BF16)] * 3
        out_specs += [row(0)] * 3
    if with_means:
        names.append("means")
        out_shapes.append(jax.ShapeDtypeStruct((m // tr, 1, width), F32))
        out_specs.append(pl.BlockSpec((None, 1, width), lambda r: (r, 0, 0)))
    outs = pl.pallas_call(
        functools.partial(_rope_kernel, head_dim=head_dim, n_stack_in=n_stack_in, with_qf=with_qf,
                          with_bf16=with_bf16, with_means=with_means),
        grid=(m // tr,),
        in_specs=in_specs,
        out_specs=out_specs,
        out_shape=out_shapes,
        input_output_aliases=aliases,
        compiler_params=_params("parallel"),
        name="rope",
    )(*args)
    return dict(zip(names, outs))


def _softmax_pv(s_blocks, v_ref):
    blk = MOBA_BLOCK
    m = jnp.max(s_blocks[0], axis=1, keepdims=True)
    for s in s_blocks[1:]:
        m = jnp.maximum(m, jnp.max(s, axis=1, keepdims=True))
    l = None
    acc = None
    for kj, s in enumerate(s_blocks):
        p = jnp.exp(s - m)
        ls = jnp.sum(p, axis=1, keepdims=True)
        pv = jnp.dot(p.astype(BF16), v_ref[kj * blk:(kj + 1) * blk, :], preferred_element_type=F32)
        l = ls if l is None else l + ls
        acc = pv if acc is None else acc + pv
    return acc / l


def _moba_prompt_kernel(qb_ref, qf_ref, k_ref, v_ref, mean_ref, o_ref):
    blk = MOBA_BLOCK
    nq = qb_ref.shape[0] // blk
    scale = HEAD_DIM ** -0.5
    means = mean_ref[...]
    causal = lax.broadcasted_iota(jnp.int32, (blk, blk), 1) <= lax.broadcasted_iota(jnp.int32, (blk, blk), 0)
    for qi in range(nq):
        rows = slice(qi * blk, (qi + 1) * blk)
        q = qb_ref[rows, :]
        sel_bias = None
        if qi > MOBA_TOPK:
            gate = lax.dot_general(qf_ref[rows, :], means, (((1,), (1,)), ((), ())),
                                   precision=lax.Precision.HIGHEST, preferred_element_type=F32)
            blk_id = lax.broadcasted_iota(jnp.int32, gate.shape, 1)
            rank = jnp.zeros(gate.shape, F32)
            for m_id in range(qi):
                gm = gate[:, m_id:m_id + 1]
                beats = (gm > gate) | ((gm == gate) & (blk_id > m_id))
                rank = rank + jnp.where(beats, 1.0, 0.0)
            sel_bias = jnp.where(rank < float(MOBA_TOPK), 0.0, NEG_INF)
        s_blocks = []
        for kj in range(qi + 1):
            s = _nt_dot(q, k_ref[kj * blk:(kj + 1) * blk, :]) * scale
            if kj == qi:
                s = jnp.where(causal, s, NEG_INF)
            elif sel_bias is not None:
                s = s + sel_bias[:, kj:kj + 1]
            s_blocks.append(s)
        o_ref[rows, :] = _softmax_pv(s_blocks, v_ref).astype(o_ref.dtype)


def _moba_prompt(qb, qf, kb, vb, means, *, batch, seq):
    m, width = qb.shape
    n_heads = width // HEAD_DIM
    nb = means.shape[1]
    spec = pl.BlockSpec((seq, HEAD_DIM), lambda b, h: (b, h))
    return pl.pallas_call(
        _moba_prompt_kernel,
        grid=(batch, n_heads),
        in_specs=[spec, spec, spec, spec, pl.BlockSpec((None, nb, HEAD_DIM), lambda b, h: (b, 0, h))],
        out_specs=spec,
        out_shape=jax.ShapeDtypeStruct((m, width), BF16),
        compiler_params=_params("parallel", "parallel"),
        name="moba_prompt",
    )(qb, qf, kb, vb, means)


def _diff_lambda(lp_ref):
    lp = lp_ref[...]
    s1 = jnp.sum(lp[0:1, :] * lp[1:2, :], axis=1, keepdims=True)
    s2 = jnp.sum(lp[2:3, :] * lp[3:4, :], axis=1, keepdims=True)
    return jnp.exp(s1) - jnp.exp(s2)


def _diff_finish(o1, o2, lam, subln, lam_init):
    o = o1 - lam * o2
    ms = jnp.mean(o * o, axis=-1, keepdims=True)
    return (o * lax.rsqrt(ms + SUBLN_EPS) * subln) * (1.0 - lam_init)


def _diff_prompt_kernel(qb_ref, k_ref, v_ref, lp_ref, subln_ref, o_ref, *, lam_init):
    blk = MOBA_BLOCK
    nq = qb_ref.shape[0] // blk
    dd = HEAD_DIM // 2
    scale = dd ** -0.5
    lam = _diff_lambda(lp_ref) + lam_init
    subln = subln_ref[...]
    lane = lax.broadcasted_iota(jnp.int32, (blk, HEAD_DIM), 1)
    r_id = lax.broadcasted_iota(jnp.int32, (2 * blk, blk), 0) % blk
    causal = lax.broadcasted_iota(jnp.int32, (2 * blk, blk), 1) <= r_id
    for qi in range(nq):
        rows = slice(qi * blk, (qi + 1) * blk)
        q = qb_ref[rows, :].astype(F32)
        q_stack = jnp.concatenate([jnp.where(lane < dd, q, 0.0), jnp.where(lane >= dd, q, 0.0)], axis=0).astype(BF16)
        s_blocks = []
        for kj in range(qi + 1):
            s = _nt_dot(q_stack, k_ref[kj * blk:(kj + 1) * blk, :]) * scale
            if kj == qi:
                s = jnp.where(causal, s, NEG_INF)
            s_blocks.append(s)
        o = _softmax_pv(s_blocks, v_ref)
        o_ref[rows, :] = _diff_finish(o[:blk], o[blk:], lam, subln, lam_init).astype(o_ref.dtype)


def _diff_prompt(qb, kb, vb, lp, subln, *, batch, seq, lam_init):
    m, width = qb.shape
    n_heads = width // HEAD_DIM
    spec = pl.BlockSpec((seq, HEAD_DIM), lambda b, h: (b, h))
    return pl.pallas_call(
        functools.partial(_diff_prompt_kernel, lam_init=lam_init),
        grid=(batch, n_heads),
        in_specs=[spec, spec, spec,
                  pl.BlockSpec(lp.shape, lambda b, h: (0, 0)),
                  pl.BlockSpec((1, HEAD_DIM), lambda b, h: (0, 0))],
        out_specs=spec,
        out_shape=jax.ShapeDtypeStruct((m, width), BF16),
        compiler_params=_params("parallel", "parallel"),
        name="diff_prompt",
    )(qb, kb, vb, lp, subln.reshape(1, HEAD_DIM))


def _mem_attn_kernel(q_ref, mk_ref, mv_ref, o_ref):
    tq = q_ref.shape[0]
    q = _pad_rows(q_ref[...], BF16_SUBLANES).astype(BF16)
    s = _nt_dot(q, mk_ref[...].astype(BF16)) * (HEAD_DIM ** -0.5)
    m = jnp.max(s, axis=1, keepdims=True)
    p = jnp.exp(s - m)
    l = jnp.sum(p, axis=1, keepdims=True)
    o = jnp.dot(p.astype(BF16), mv_ref[...].astype(BF16), preferred_element_type=F32)
    o_ref[...] = (o / l)[:tq].astype(o_ref.dtype)


def _mem_attn(proj, mem_k, mem_v, *, batch, seq, q_col_block, kv_col_blocks):
    m = proj.shape[0]
    n_mem = mem_k.shape[1]
    n_h = (proj.shape[1] // HEAD_DIM) - q_col_block
    tq = _pick(seq, (512, 256, 128, 64, 32, 16, 8))
    nq = seq // tq
    k_spec = pl.BlockSpec((None, n_mem, HEAD_DIM), lambda b, i, h: (b, 0, kv_col_blocks[0] + h))
    v_spec = pl.BlockSpec((None, n_mem, HEAD_DIM), lambda b, i, h: (b, 0, kv_col_blocks[1] + h))
    return pl.pallas_call(
        _mem_attn_kernel,
        grid=(batch, nq, n_h),
        in_specs=[pl.BlockSpec((tq, HEAD_DIM), lambda b, i, h: (b * nq + i, q_col_block + h)), k_spec, v_spec],
        out_specs=pl.BlockSpec((tq, HEAD_DIM), lambda b, i, h: (b * nq + i, h)),
        out_shape=jax.ShapeDtypeStruct((m, n_h * HEAD_DIM), BF16 if tq % BF16_SUBLANES == 0 else F32),
        compiler_params=_params("parallel", "parallel", "parallel"),
        name="mem_attn",
    )(proj, mem_k, mem_v)


def _ffn_up_kernel(x_ref, ln_ref, wg_ref, wa_ref, st_ref, cw_ref, cb_ref, act_ref, nst_ref, h_ref, *, seq, chunk):
    @pl.when(pl.program_id(1) == 0)
    def _():
        x = x_ref[...]
        ms = jnp.mean(x * x, axis=-1, keepdims=True)
        h_ref[...] = (x * lax.rsqrt(ms + EPS) * ln_ref[...]).astype(BF16)

    tm = h_ref.shape[0]
    w = cw_ref[...]
    bias = cb_ref[...]
    wg = wg_ref[...].astype(BF16)
    wa = wa_ref[...].astype(BF16)

    def gate(g0, g1, g2, a):
        c = bias + w[0:1, :] * g2 + w[1:2, :] * g1 + w[2:3, :] * g0
        return (c / (1.0 + jnp.exp(-c)) * a).astype(act_ref.dtype)

    carry = None
    for r in range(tm // chunk):
        rows = slice(r * chunk, (r + 1) * chunk)
        h = h_ref[rows, :]
        g = jnp.dot(h, wg, preferred_element_type=F32)
        a = jnp.dot(h, wa, preferred_element_type=F32)
        g1 = pltpu.roll(g, 1, 0)
        g2 = pltpu.roll(g, 2, 0)
        if seq % chunk == 0:
            if (r * chunk) % seq == 0:
                carry = st_ref[(r * chunk) // seq]
            act_ref[rows, :] = gate(g, g1, g2, a)
            top = slice(r * chunk, r * chunk + BF16_SUBLANES)
            row = lax.broadcasted_iota(jnp.int32, (BF16_SUBLANES, g.shape[1]), 0)
            gt = g[:BF16_SUBLANES]
            t1 = jnp.where(row == 0, carry[1:2, :], pltpu.roll(gt, 1, 0))
            t2 = jnp.where(row == 0, carry[0:1, :], jnp.where(row == 1, carry[1:2, :], pltpu.roll(gt, 2, 0)))
            act_ref[top, :] = gate(gt, t1, t2, a[:BF16_SUBLANES])
            carry = g[chunk - (CONV_W - 1):, :]
            if ((r + 1) * chunk) % seq == 0:
                nst_ref[((r + 1) * chunk) // seq - 1] = carry
        else:
            row = lax.broadcasted_iota(jnp.int32, g.shape, 0)
            for b in range(chunk // seq):
                st = st_ref[r * (chunk // seq) + b]
                g1 = jnp.where(row == b * seq, st[1:2, :], g1)
                g2 = jnp.where(row == b * seq, st[0:1, :], jnp.where(row == b * seq + 1, st[1:2, :], g2))
                nst_ref[r * (chunk // seq) + b] = g[(b + 1) * seq - (CONV_W - 1):(b + 1) * seq, :]
            act_ref[rows, :] = gate(g, g1, g2, a)


def _ffn_up(x, ln, w_stack, layer, state, conv_w, conv_b, *, batch, seq):
    m, d = x.shape
    d_ff = w_stack.shape[2] // 2
    assert seq >= CONV_W - 1
    tm = seq * _pick(batch, (max(1, FFN_ROW_TILE // seq),))
    n_seq = tm // seq
    if seq % BF16_SUBLANES == 0:
        chunk = _pick(seq, (FFN_ROW_CHUNK,))
    else:
        chunk = tm
    assert seq % chunk == 0 or chunk % seq == 0
    tn = _pick(d_ff, (256, 128)) if tm > 1024 else _pick(d_ff, (512, 256, 128))
    nc = d_ff // tn
    x_kwargs = dict(pipeline_mode=pl.Buffered(1)) if tm > 1024 else {}
    return pl.pallas_call(
        functools.partial(_ffn_up_kernel, seq=seq, chunk=chunk),
        grid=(m // tm, nc),
        in_specs=[
            pl.BlockSpec((tm, d), lambda i, j: (i, 0), **x_kwargs),
            pl.BlockSpec((1, d), lambda i, j: (0, 0)),
            pl.BlockSpec((None, d, tn), lambda i, j: (layer, 0, j)),
            pl.BlockSpec((None, d, tn), lambda i, j: (layer, 0, nc + j)),
            pl.BlockSpec((n_seq, CONV_W - 1, tn), lambda i, j: (i, 0, j)),
            pl.BlockSpec((CONV_W, tn), lambda i, j: (0, j)),
            pl.BlockSpec((1, tn), lambda i, j: (0, j)),
        ],
        out_specs=[
            pl.BlockSpec((tm, tn), lambda i, j: (i, j)),
            pl.BlockSpec((n_seq, CONV_W - 1, tn), lambda i, j: (i, 0, j)),
        ],
        out_shape=[
            jax.ShapeDtypeStruct((m, d_ff), BF16),
            jax.ShapeDtypeStruct((batch, CONV_W - 1, d_ff), F32),
        ],
        scratch_shapes=[pltpu.VMEM((tm, d), BF16)],
        compiler_params=_params("parallel", "arbitrary"),
        name="ffn_up",
    )(x, ln.reshape(1, d), w_stack, w_stack, state, conv_w, conv_b.reshape(1, d_ff))


def _page_mean_kernel(pt_ref, *refs):
    del pt_ref
    o_ref = refs[-1]
    for g in range(o_ref.shape[0]):
        s = jnp.sum(refs[2 * g][...], axis=0, keepdims=True) + jnp.sum(refs[2 * g + 1][...], axis=0, keepdims=True)
        o_ref[g:g + 1, :] = s * (1.0 / MOBA_BLOCK)


def _page_means(cache_k, page_table_flat, layer, *, batch, n_pages):
    page, width = cache_k.shape[2:]
    assert MOBA_BLOCK == 2 * page
    n_blocks = n_pages // 2
    nbs = _pick(n_blocks, (BLOCKS_PER_STEP, 2, 1))
    n_steps = n_blocks // nbs

    def k_spec(j):
        return pl.BlockSpec((None, None, page, width),
                            lambda b, n, pt: (layer, pt[b * n_pages + 2 * nbs * n + j], 0, 0))

    out = pl.pallas_call(
        _page_mean_kernel,
        grid_spec=pltpu.PrefetchScalarGridSpec(
            num_scalar_prefetch=1,
            grid=(batch, n_steps),
            in_specs=[k_spec(j) for j in range(2 * nbs)],
            out_specs=pl.BlockSpec((None, None, nbs, width), lambda b, n, pt: (b, n, 0, 0)),
        ),
        out_shape=jax.ShapeDtypeStruct((batch, n_steps, nbs, width), F32),
        compiler_params=_params("parallel", "parallel"),
        name="page_means",
    )(page_table_flat, *([cache_k] * (2 * nbs)))
    return out.reshape(batch, n_blocks, width)


def _topk_kernel(q_ref, mean_ref, o_ref):
    n_heads = q_ref.shape[1] // HEAD_DIM
    n_blocks = mean_ref.shape[0]
    for h in range(n_heads):
        sl = slice(h * HEAD_DIM, (h + 1) * HEAD_DIM)
        gate = lax.dot_general(q_ref[:, sl], mean_ref[:, sl], (((1,), (1,)), ((), ())),
                               precision=lax.Precision.HIGHEST, preferred_element_type=F32)
        lane = lax.broadcasted_iota(jnp.int32, gate.shape, 1)
        out_lane = lax.broadcasted_iota(jnp.int32, (gate.shape[0], HEAD_DIM), 1)
        out = jnp.zeros((gate.shape[0], HEAD_DIM), jnp.int32)
        for r in range(MOBA_TOPK):
            mx = jnp.max(gate, axis=1, keepdims=True)
            idx = jnp.min(jnp.where(gate == mx, lane, n_blocks), axis=1, keepdims=True)
            out = jnp.where(out_lane == r, idx, out)
            gate = jnp.where(lane == idx, NEG_INF, gate)
        o_ref[h] = out


def _sample_topk(qf, means, *, batch, seq):
    width = qf.shape[1]
    n_heads = width // HEAD_DIM
    n_blocks = means.shape[1]
    return pl.pallas_call(
        _topk_kernel,
        grid=(batch,),
        in_specs=[pl.BlockSpec((seq, width), lambda b: (b, 0)),
                  pl.BlockSpec((None, n_blocks, width), lambda b: (b, 0, 0))],
        out_specs=pl.BlockSpec((None, n_heads, seq, HEAD_DIM), lambda b: (b, 0, 0, 0)),
        out_shape=jax.ShapeDtypeStruct((batch, n_heads, seq, HEAD_DIM), jnp.int32),
        compiler_params=_params("parallel"),
        name="sample_topk",
    )(qf, means)


def _moba_sample_kernel(pt_ref, idx_ref, q_ref, kn_ref, vn_ref, *refs, nqs):
    del pt_ref, idx_ref
    n_tiles = 2 * MOBA_TOPK
    k_refs = refs[:nqs * n_tiles]
    v_refs = refs[nqs * n_tiles:2 * nqs * n_tiles]
    o_ref = refs[2 * nqs * n_tiles]
    c0 = pl.program_id(2) * nqs
    scale = HEAD_DIM ** -0.5
    seq = q_ref.shape[0]
    k_own = kn_ref[...]
    v_own = vn_ref[...]
    key_id = lax.broadcasted_iota(jnp.int32, (seq, 1), 0)

    @pl.when(c0 == 0)
    def _():
        o_ref[...] = jnp.zeros_like(o_ref)

    out = o_ref[...]
    row = lax.broadcasted_iota(jnp.int32, out.shape, 0)
    for j in range(nqs):
        c = c0 + j
        q_row = q_ref[pl.ds(c, 1), :]
        s_own = jnp.sum(k_own * q_row, axis=1, keepdims=True) * scale
        s_own = jnp.where(key_id <= c, s_own, NEG_INF)
        s_past = [jnp.sum(k_ref[...] * q_row, axis=1, keepdims=True) * scale
                  for k_ref in k_refs[j * n_tiles:(j + 1) * n_tiles]]
        m = jnp.max(s_own, axis=0, keepdims=True)
        for s in s_past:
            m = jnp.maximum(m, jnp.max(s, axis=0, keepdims=True))
        p_own = jnp.exp(s_own - m)
        l = jnp.sum(p_own, axis=0, keepdims=True)
        acc = jnp.sum(p_own * v_own, axis=0, keepdims=True)
        for s, v_ref in zip(s_past, v_refs[j * n_tiles:(j + 1) * n_tiles]):
            p = jnp.exp(s - m)
            l = l + jnp.sum(p, axis=0, keepdims=True)
            acc = acc + jnp.sum(p * v_ref[...], axis=0, keepdims=True)
        out = jnp.where(row == c, acc / l, out)
    o_ref[...] = out


def _moba_sample(qf, k_new, v_new, cache_k, cache_v, page_table_flat, idx_flat, layer, *, batch, seq, n_pages):
    width = qf.shape[1]
    n_heads = width // HEAD_DIM
    page = cache_k.shape[2]
    assert n_pages // 2 >= MOBA_TOPK
    nqs = _pick(seq, (QUERIES_PER_STEP, 2, 1))

    def kv_spec(j, slot, half):
        def imap(b, h, c, pt, idx):
            blk = idx[((b * n_heads + h) * seq + c * nqs + j) * MOBA_TOPK + slot]
            return (layer, pt[b * n_pages + 2 * blk + half], 0, h)
        return pl.BlockSpec((None, None, page, HEAD_DIM), imap)

    tiles = [kv_spec(j, s, half) for j in range(nqs) for s in range(MOBA_TOPK) for half in range(2)]
    new_spec = pl.BlockSpec((seq, HEAD_DIM), lambda b, h, c, pt, idx: (b, h))
    return pl.pallas_call(
        functools.partial(_moba_sample_kernel, nqs=nqs),
        grid_spec=pltpu.PrefetchScalarGridSpec(
            num_scalar_prefetch=2,
            grid=(batch, n_heads, seq // nqs),
            in_specs=[new_spec, new_spec, new_spec] + tiles + tiles,
            out_specs=new_spec,
        ),
        out_shape=jax.ShapeDtypeStruct((batch * seq, width), F32),
        compiler_params=_params("parallel", "parallel", "arbitrary"),
        name="moba_sample",
    )(page_table_flat, idx_flat, qf, k_new, v_new, *([cache_k] * len(tiles)), *([cache_v] * len(tiles)))


def _diff_sample_kernel(pt_ref, q_ref, kn_ref, vn_ref, *refs, n_pg, lam_init):
    del pt_ref
    k_refs = refs[:n_pg]
    v_refs = refs[n_pg:2 * n_pg]
    lp_ref, subln_ref, o_ref, qbd_ref, m_ref, l_ref, acc_ref = refs[2 * n_pg:]
    p_id = pl.program_id(1)
    n_p = pl.num_programs(1)
    seq, width = q_ref.shape
    page = k_refs[0].shape[0]
    dd = HEAD_DIM // 2
    n_sub = width // dd
    n_heads = width // HEAD_DIM
    scale = dd ** -0.5
    g = 2 * seq

    @pl.when(p_id == 0)
    def _():
        q = q_ref[...]
        q_rep = jnp.concatenate([q] * n_sub, axis=0)
        r_sub = lax.broadcasted_iota(jnp.int32, q_rep.shape, 0) // seq
        c_sub = lax.broadcasted_iota(jnp.int32, q_rep.shape, 1) // dd
        qbd_ref[...] = jnp.where(r_sub == c_sub, q_rep, 0.0).astype(BF16)
        m_ref[...] = jnp.full(m_ref.shape, NEG_INF, F32)
        l_ref[...] = jnp.zeros(l_ref.shape, F32)
        acc_ref[...] = jnp.zeros(acc_ref.shape, F32)

    def update(s_list, v_list):
        m_old = m_ref[...]
        m_new = m_old
        for s in s_list:
            m_new = jnp.maximum(m_new, jnp.max(s, axis=1, keepdims=True))
        alpha = jnp.exp(m_old - m_new)
        p_list = [jnp.exp(s - m_new).astype(BF16) for s in s_list]
        l_new = alpha * l_ref[...]
        for p in p_list:
            l_new = l_new + jnp.sum(p.astype(F32), axis=1, keepdims=True)
        l_ref[...] = l_new
        m_ref[...] = m_new
        for h in range(n_heads):
            hr = slice(h * g, (h + 1) * g)
            hc = slice(h * HEAD_DIM, (h + 1) * HEAD_DIM)
            acc = alpha[hr, :] * acc_ref[hr, :]
            for p, v in zip(p_list, v_list):
                acc = acc + jnp.dot(p[hr, :], v[:, hc], preferred_element_type=F32)
            acc_ref[hr, :] = acc

    qbd = qbd_ref[...]
    update([_nt_dot(qbd, k_ref[...].astype(BF16)) * scale for k_ref in k_refs],
           [v_ref[...].astype(BF16) for v_ref in v_refs])

    @pl.when(p_id == n_p - 1)
    def _():
        s = _nt_dot(qbd_ref[...], _pad_rows(kn_ref[...], page).astype(BF16)) * scale
        q_pos = lax.broadcasted_iota(jnp.int32, s.shape, 0) % seq
        k_pos = lax.broadcasted_iota(jnp.int32, s.shape, 1)
        update([jnp.where(k_pos <= q_pos, s, NEG_INF)], [_pad_rows(vn_ref[...], page).astype(BF16)])
        o = acc_ref[...] / l_ref[...]
        lam = _diff_lambda(lp_ref) + lam_init
        subln = subln_ref[...]
        for h in range(n_heads):
            o1 = o[h * g:h * g + seq, :]
            o2 = o[h * g + seq:(h + 1) * g, :]
            o_ref[:, h * HEAD_DIM:(h + 1) * HEAD_DIM] = _diff_finish(o1, o2, lam, subln, lam_init)


def _diff_sample(qf, k_new, v_new, cache_k, cache_v, page_table_flat, lp, subln, layer, *,
                 batch, seq, n_pages, lam_init):
    width = qf.shape[1]
    page = cache_k.shape[2]
    rows = (width // (HEAD_DIM // 2)) * seq
    n_pg = _pick(n_pages, (PAGES_PER_STEP, 2, 1))
    new_spec = pl.BlockSpec((seq, width), lambda b, p, pt: (b, 0))

    def kv_spec(j):
        return pl.BlockSpec((None, None, page, width),
                            lambda b, p, pt: (layer, pt[b * n_pages + n_pg * p + j], 0, 0))

    kv_specs = [kv_spec(j) for j in range(n_pg)]
    return pl.pallas_call(
        functools.partial(_diff_sample_kernel, n_pg=n_pg, lam_init=lam_init),
        grid_spec=pltpu.PrefetchScalarGridSpec(
            num_scalar_prefetch=1,
            grid=(batch, n_pages // n_pg),
            in_specs=[new_spec, new_spec, new_spec] + kv_specs + kv_specs +
                     [pl.BlockSpec(lp.shape, lambda b, p, pt: (0, 0)),
                      pl.BlockSpec((1, HEAD_DIM), lambda b, p, pt: (0, 0))],
            out_specs=new_spec,
            scratch_shapes=[pltpu.VMEM((rows, width), BF16), pltpu.VMEM((rows, 1), F32),
                            pltpu.VMEM((rows, 1), F32), pltpu.VMEM((rows, HEAD_DIM), F32)],
        ),
        out_shape=jax.ShapeDtypeStruct((batch * seq, width), F32),
        compiler_params=_params("parallel", "arbitrary"),
        name="diff_sample",
    )(page_table_flat, qf, k_new, v_new, *([cache_k] * n_pg), *([cache_v] * n_pg), lp, subln.reshape(1, HEAD_DIM))


def _layer(i, x, pos, mem_k, mem_v, mem_cols, conv_state, wts, *, batch, seq, paged=None, stack=None):
    width = wts["self_width"]
    is_moba = i % N_MIXERS == 0
    prompt = paged is None
    head_dim = HEAD_DIM if is_moba else HEAD_DIM // 2
    proj = _norm_matmul(x, wts["ln_attn"][i], wts["w_in"], i)
    cos_t, sin_t = _rope_tables(pos, head_dim)
    r = _rope(proj, cos_t, sin_t, seq=seq, width=width, head_dim=head_dim, with_qf=is_moba or not prompt,
              with_bf16=prompt, with_means=is_moba and prompt, stack=stack)
    if is_moba:
        if prompt:
            means = r["means"].reshape(batch, seq // MOBA_BLOCK, width)
            self_out = _moba_prompt(r["qb"], r["q"], r["kb"], r["vb"], means, batch=batch, seq=seq)
        else:
            cache_k, cache_v, pt_flat, n_pages = paged
            means = _page_means(cache_k, pt_flat, i, batch=batch, n_pages=n_pages)
            idx = _sample_topk(r["q"], means, batch=batch, seq=seq)
            self_out = _moba_sample(r["q"], r["k"], r["v"], cache_k, cache_v, pt_flat,
                                    idx[..., :MOBA_TOPK].reshape(-1), i,
                                    batch=batch, seq=seq, n_pages=n_pages).astype(BF16)
    else:
        j = i // N_MIXERS
        lam_init = 0.8 - 0.6 * math.exp(-0.3 * i)
        lp = wts["diff_lambda"][j]
        subln = wts["diff_subln"][j]
        if prompt:
            self_out = _diff_prompt(r["qb"], r["kb"], r["vb"], lp, subln, batch=batch, seq=seq, lam_init=lam_init)
        else:
            cache_k, cache_v, pt_flat, n_pages = paged
            self_out = _diff_sample(r["q"], r["k"], r["v"], cache_k, cache_v, pt_flat, lp, subln, i,
                                    batch=batch, seq=seq, n_pages=n_pages, lam_init=lam_init).astype(BF16)
    mem_out = _mem_attn(proj, mem_k, mem_v, batch=batch, seq=seq, q_col_block=3 * width // HEAD_DIM,
                        kv_col_blocks=mem_cols)
    x = _matmul_res([self_out, mem_out.astype(BF16)], wts["w_o"], i, x)
    act, new_conv = _ffn_up(x, wts["ln_ffn"][i], wts["w_up"], i, conv_state, wts["conv_w"][i], wts["conv_b"][i],
                            batch=batch, seq=seq)
    x = _matmul_res([act], wts["w_down"], i, x, tn_prefs=(256, 128))
    return x, r["k"], r["v"], new_conv


def kernel(x_prompt, x_sample, mem_prompt, cache_k, cache_v, cache_mem_k, cache_mem_v, state_conv, page_table,
           ln_attn, w_in, w_o, diff_lambda, diff_subln, ln_mem, w_mem_kv, ln_ffn, w_up, conv_w, conv_b,
           w_down, ln_final):
    bp, tp, d_model = x_prompt.shape
    bs, ts, _ = x_sample.shape
    depth = w_in.shape[0]
    n_mem = mem_prompt.shape[1]
    mem_width = w_mem_kv.shape[2] // 2
    self_width = d_model - mem_width
    d_ff = w_down.shape[1]
    n_pages = page_table.shape[1]
    page = cache_k.shape[2]
    past = n_pages * page
    assert tp % MOBA_BLOCK == 0 and past % MOBA_BLOCK == 0 and ts <= MOBA_BLOCK

    wts = dict(self_width=self_width, ln_attn=ln_attn, ln_ffn=ln_ffn, diff_lambda=diff_lambda,
               diff_subln=diff_subln, conv_w=conv_w, conv_b=conv_b, w_in=w_in, w_o=w_o, w_up=w_up, w_down=w_down)
    pos_p = jnp.arange(tp, dtype=jnp.int32)
    pos_s = past + jnp.arange(ts, dtype=jnp.int32)
    conv0 = jnp.zeros((bp, CONV_W - 1, d_ff), F32)
    pt_flat = page_table.reshape(-1).astype(jnp.int32)
    paged = (cache_k, cache_v, pt_flat, n_pages)
    mem_heads = mem_width // HEAD_DIM

    xp = x_prompt.reshape(bp * tp, d_model)
    xs = x_sample.reshape(bs * ts, d_model)
    mem_rows = mem_prompt.reshape(bp * n_mem, d_model)
    k_stack = v_stack = None
    mkv_l, cp_l, ks_l, vs_l, cs_l = [], [], [], [], []
    for i in range(depth):
        mem_kv = _norm_matmul(mem_rows, ln_mem[i], w_mem_kv, i).reshape(bp, n_mem, 2 * mem_width)
        xp, k_stack, v_stack, cp = _layer(i, xp, pos_p, mem_kv, mem_kv, (0, mem_heads), conv0, wts,
                                          batch=bp, seq=tp, stack=(i, depth, k_stack, v_stack))
        xs, k_s, v_s, c_s = _layer(i, xs, pos_s, cache_mem_k[i], cache_mem_v[i], (0, 0), state_conv[i], wts,
                                   batch=bs, seq=ts, paged=paged)
        mkv_l.append(mem_kv)
        cp_l.append(cp)
        ks_l.append(k_s.reshape(bs, ts, self_width))
        vs_l.append(v_s.reshape(bs, ts, self_width))
        cs_l.append(c_s)
    y_prompt = _rmsnorm(xp, ln_final).reshape(bp, tp, d_model)
    y_sample = _rmsnorm(xs, ln_final).reshape(bs, ts, d_model)
    mkv = jnp.stack(mkv_l)
    return (y_prompt, y_sample,
            k_stack.reshape(depth, bp, tp, self_width), v_stack.reshape(depth, bp, tp, self_width),
            mkv[..., :mem_width], mkv[..., mem_width:], jnp.stack(cp_l),
            jnp.stack(ks_l), jnp.stack(vs_l), jnp.stack(cs_l))
```

```python
import functools
import math

import jax
import jax.numpy as jnp
from jax import lax
from jax.experimental import pallas as pl
from jax.experimental.pallas import tpu as pltpu

F32 = jnp.float32
BF16 = jnp.bfloat16

HEAD_DIM = 128
MOBA_BLOCK = 256
MOBA_TOPK = 3
CONV_W = 3
ROPE_THETA = 10000.0
EPS = 1e-6
SUBLN_EPS = 1e-5
N_MIXERS = 2

VMEM_LIMIT_BYTES = 54 * 1024 * 1024
NEG_INF = float("-inf")
LOG2_E = 1.4426950408889634
BF16_SUBLANES = 16
FFN_ROW_TILE = 2048
FFN_ROW_CHUNK = 512
PAGES_PER_STEP = 8
BLOCKS_PER_STEP = 4
QUERIES_PER_STEP = 4


def _params(*sem):
    return pltpu.CompilerParams(dimension_semantics=sem, vmem_limit_bytes=VMEM_LIMIT_BYTES)


def _pick(n, prefs):
    for p in prefs:
        if n % p == 0:
            return p
    return n


def _nt_dot(a, b):
    return lax.dot_general(a, b, (((1,), (1,)), ((), ())), preferred_element_type=F32)


def _pad_rows(x, rows):
    if x.shape[0] >= rows:
        return x
    return jnp.concatenate([x, jnp.zeros((rows - x.shape[0],) + x.shape[1:], x.dtype)], axis=0)


def _norm_matmul_kernel(x_ref, g_ref, w_ref, o_ref, h_ref):
    @pl.when(pl.program_id(1) == 0)
    def _():
        x = x_ref[...]
        ms = jnp.mean(x * x, axis=-1, keepdims=True)
        h_ref[...] = (x * lax.rsqrt(ms + EPS) * g_ref[...]).astype(BF16)

    o_ref[...] = jnp.dot(h_ref[...], w_ref[...].astype(BF16), preferred_element_type=F32)


def _norm_matmul(x, g, w_stack, layer):
    m, d = x.shape
    n = w_stack.shape[2]
    tm = _pick(m, (2048, 1024, 512, 256, 128, 64))
    tn = _pick(n, (512, 256, 128))
    x_kwargs = dict(pipeline_mode=pl.Buffered(1)) if tm > 1024 else {}
    return pl.pallas_call(
        _norm_matmul_kernel,
        grid=(m // tm, n // tn),
        in_specs=[
            pl.BlockSpec((tm, d), lambda i, j: (i, 0), **x_kwargs),
            pl.BlockSpec((1, d), lambda i, j: (0, 0)),
            pl.BlockSpec((None, d, tn), lambda i, j: (layer, 0, j)),
        ],
        out_specs=pl.BlockSpec((tm, tn), lambda i, j: (i, j)),
        out_shape=jax.ShapeDtypeStruct((m, n), F32),
        scratch_shapes=[pltpu.VMEM((tm, d), BF16)],
        compiler_params=_params("parallel", "arbitrary"),
        name="norm_matmul",
    )(x, g.reshape(1, d), w_stack)


def _matmul_res_kernel(*refs, n_parts):
    a_refs = refs[:n_parts]
    w_refs = refs[n_parts:2 * n_parts]
    r_ref, o_ref = refs[2 * n_parts:]
    acc = r_ref[...]
    for a_ref, w_ref in zip(a_refs, w_refs):
        acc = acc + jnp.dot(a_ref[...], w_ref[...].astype(BF16), preferred_element_type=F32)
    o_ref[...] = acc


def _matmul_res(a_parts, w_stack, layer, res, *, tm_prefs=(2048, 1024, 512, 256, 128, 64), tn_prefs=(512, 256, 128)):
    m, n = res.shape
    tm = _pick(m, tm_prefs)
    tn = _pick(n, tn_prefs)
    a_kwargs = dict(pipeline_mode=pl.Buffered(1)) if tm > 1024 else {}
    a_specs, w_specs = [], []
    row0 = 0
    for a in a_parts:
        k = a.shape[1]
        assert row0 % k == 0
        a_specs.append(pl.BlockSpec((tm, k), lambda i, j: (i, 0), **a_kwargs))
        w_specs.append(pl.BlockSpec((None, k, tn), lambda i, j, rb=row0 // k: (layer, rb, j)))
        row0 += k
    assert row0 == w_stack.shape[1]
    io_spec = pl.BlockSpec((tm, tn), lambda i, j: (i, j))
    return pl.pallas_call(
        functools.partial(_matmul_res_kernel, n_parts=len(a_parts)),
        grid=(m // tm, n // tn),
        in_specs=a_specs + w_specs + [io_spec],
        out_specs=io_spec,
        out_shape=jax.ShapeDtypeStruct((m, n), F32),
        compiler_params=_params("parallel", "parallel"),
        name="matmul_res",
    )(*a_parts, *([w_stack] * len(a_parts)), res)


def _rmsnorm_kernel(x_ref, g_ref, o_ref):
    x = x_ref[...]
    ms = jnp.mean(x * x, axis=-1, keepdims=True)
    o_ref[...] = x * lax.rsqrt(ms + EPS) * g_ref[...]


def _rmsnorm(x, g):
    m, d = x.shape
    tm = _pick(m, (512, 256, 128, 64))
    return pl.pallas_call(
        _rmsnorm_kernel,
        grid=(m // tm,),
        in_specs=[pl.BlockSpec((tm, d), lambda i: (i, 0)), pl.BlockSpec((1, d), lambda i: (0, 0))],
        out_specs=pl.BlockSpec((tm, d), lambda i: (i, 0)),
        out_shape=jax.ShapeDtypeStruct((m, d), F32),
        compiler_params=_params("parallel"),
        name="rmsnorm",
    )(x, g.reshape(1, d))


def _rope_tables(pos, head_dim):
    half = head_dim // 2
    inv = jnp.exp(-math.log(ROPE_THETA) * jnp.arange(half, dtype=F32) / half)
    ang = pos.astype(F32)[:, None] * inv[None, :]
    cos, sin = jnp.cos(ang), jnp.sin(ang)
    reps = HEAD_DIM // head_dim
    cos_t = jnp.tile(jnp.concatenate([cos, cos], axis=-1), (1, reps))
    sin_t = jnp.tile(jnp.concatenate([-sin, sin], axis=-1), (1, reps))
    return cos_t, sin_t


def _rope_kernel(*refs, head_dim, n_stack_in, with_qf, with_bf16, with_means):
    qb_scale = head_dim ** -0.5 * LOG2_E
    refs = refs[n_stack_in:]
    q_ref, k_ref, v_ref, cos_ref, sin_ref = refs[:5]
    outs = list(refs[5:])
    kf_ref, vf_ref = outs[:2]
    outs = outs[2:]
    qf_ref = outs.pop(0) if with_qf else None
    if with_bf16:
        qb_ref, kb_ref, vb_ref = outs[:3]
        outs = outs[3:]
    mean_ref = outs.pop(0) if with_means else None
    cos = cos_ref[...]
    sin = sin_ref[...]
    half = head_dim // 2
    n_groups = q_ref.shape[1] // HEAD_DIM
    if head_dim != HEAD_DIM:
        lane = lax.broadcasted_iota(jnp.int32, cos.shape, 1)
        first_half = (lane % head_dim) < half

    def rot(x):
        if head_dim == HEAD_DIM:
            partner = pltpu.roll(x, half, 1)
        else:
            partner = jnp.where(first_half, pltpu.roll(x, HEAD_DIM - half, 1), pltpu.roll(x, half, 1))
        return x * cos + partner * sin

    for h in range(n_groups):
        sl = slice(h * HEAD_DIM, (h + 1) * HEAD_DIM)
        qr = rot(q_ref[:, sl])
        kr = rot(k_ref[:, sl])
        kf_ref[:, sl] = kr
        if with_qf:
            qf_ref[:, sl] = qr
        if with_bf16:
            qb_ref[:, sl] = (qr * qb_scale).astype(BF16)
            kb_ref[:, sl] = kr.astype(BF16)
        if with_means:
            mean_ref[:, sl] = jnp.sum(kr, axis=0, keepdims=True) * (1.0 / MOBA_BLOCK)
    v = v_ref[...]
    vf_ref[...] = v
    if with_bf16:
        vb_ref[...] = v.astype(BF16)


def _rope(proj, cos_t, sin_t, *, seq, width, head_dim, with_qf, with_bf16, with_means, stack=None):
    m = proj.shape[0]
    tr = MOBA_BLOCK if seq % MOBA_BLOCK == 0 else seq
    n_t = seq // tr
    if with_means:
        assert tr == MOBA_BLOCK
    row = lambda c: pl.BlockSpec((tr, width), lambda r: (r, c))
    tab = pl.BlockSpec((tr, HEAD_DIM), lambda r: (r % n_t, 0))
    in_specs = [row(0), row(1), row(2), tab, tab]
    args = [proj, proj, proj, cos_t, sin_t]
    aliases = {}
    if stack is None:
        names = ["k", "v"]
        out_shapes = [jax.ShapeDtypeStruct((m, width), F32)] * 2
        out_specs = [row(0)] * 2
        n_stack_in = 0
    else:
        layer, depth, k_stack, v_stack = stack
        names = ["k", "v"]
        out_shapes = [jax.ShapeDtypeStruct((depth, m, width), F32)] * 2
        out_specs = [pl.BlockSpec((None, tr, width), lambda r: (layer, r, 0))] * 2
        n_stack_in = 0
        if k_stack is not None:
            n_stack_in = 2
            in_specs = [pl.BlockSpec(memory_space=pl.ANY)] * 2 + in_specs
            args = [k_stack, v_stack] + args
            aliases = {0: 0, 1: 1}
    if with_qf:
        names.append("q")
        out_shapes.append(jax.ShapeDtypeStruct((m, width), F32))
        out_specs.append(row(0))
    if with_bf16:
        names += ["qb", "kb", "vb"]
        out_shapes += [jax.ShapeDtypeStruct((m, width), BF16)] * 3
        out_specs += [row(0)] * 3
    if with_means:
        names.append("means")
        out_shapes.append(jax.ShapeDtypeStruct((m // tr, 1, width), F32))
        out_specs.append(pl.BlockSpec((None, 1, width), lambda r: (r, 0, 0)))
    outs = pl.pallas_call(
        functools.partial(_rope_kernel, head_dim=head_dim, n_stack_in=n_stack_in, with_qf=with_qf,
                          with_bf16=with_bf16, with_means=with_means),
        grid=(m // tr,),
        in_specs=in_specs,
        out_specs=out_specs,
        out_shape=out_shapes,
        input_output_aliases=aliases,
        compiler_params=_params("parallel"),
        name="rope",
    )(*args)
    return dict(zip(names, outs))


def _softmax_pv(s_blocks, v_ref):
    blk = MOBA_BLOCK
    lanes = HEAD_DIM
    tiles = [s[:, t * lanes:(t + 1) * lanes] for s in s_blocks for t in range(blk // lanes)]
    m_part = tiles[0]
    for t in tiles[1:]:
        m_part = jnp.maximum(m_part, t)
    m = jnp.max(m_part, axis=1, keepdims=True)
    l_part = None
    acc = None
    for kj, s in enumerate(s_blocks):
        p = jnp.exp2(s - m)
        for t in range(blk // lanes):
            pt = p[:, t * lanes:(t + 1) * lanes]
            l_part = pt if l_part is None else l_part + pt
        pv = jnp.dot(p.astype(BF16), v_ref[kj * blk:(kj + 1) * blk, :], preferred_element_type=F32)
        acc = pv if acc is None else acc + pv
    return acc / jnp.sum(l_part, axis=1, keepdims=True)


def _moba_prompt_kernel(qb_ref, qf_ref, k_ref, v_ref, mean_ref, o_ref):
    blk = MOBA_BLOCK
    nq = qb_ref.shape[0] // blk
    means = mean_ref[...]
    causal = lax.broadcasted_iota(jnp.int32, (blk, blk), 1) <= lax.broadcasted_iota(jnp.int32, (blk, blk), 0)
    for qi in range(nq):
        rows = slice(qi * blk, (qi + 1) * blk)
        q = qb_ref[rows, :]
        sel_bias = None
        if qi > MOBA_TOPK:
            gate = lax.dot_general(qf_ref[rows, :], means, (((1,), (1,)), ((), ())),
                                   precision=lax.Precision.HIGHEST, preferred_element_type=F32)
            blk_id = lax.broadcasted_iota(jnp.int32, gate.shape, 1)
            rank = jnp.zeros(gate.shape, F32)
            for m_id in range(qi):
                gm = gate[:, m_id:m_id + 1]
                beats = (gm > gate) | ((gm == gate) & (blk_id > m_id))
                rank = rank + jnp.where(beats, 1.0, 0.0)
            sel_bias = jnp.where(rank < float(MOBA_TOPK), 0.0, NEG_INF)
        s_blocks = []
        for kj in range(qi + 1):
            s = _nt_dot(q, k_ref[kj * blk:(kj + 1) * blk, :])
            if kj == qi:
                s = jnp.where(causal, s, NEG_INF)
            elif sel_bias is not None:
                s = s + sel_bias[:, kj:kj + 1]
            s_blocks.append(s)
        o_ref[rows, :] = _softmax_pv(s_blocks, v_ref).astype(o_ref.dtype)


def _moba_prompt(qb, qf, kb, vb, means, *, batch, seq):
    m, width = qb.shape
    n_heads = width // HEAD_DIM
    nb = means.shape[1]
    spec = pl.BlockSpec((seq, HEAD_DIM), lambda b, h: (b, h))
    return pl.pallas_call(
        _moba_prompt_kernel,
        grid=(batch, n_heads),
        in_specs=[spec, spec, spec, spec, pl.BlockSpec((None, nb, HEAD_DIM), lambda b, h: (b, 0, h))],
        out_specs=spec,
        out_shape=jax.ShapeDtypeStruct((m, width), BF16),
        compiler_params=_params("parallel", "parallel"),
        name="moba_prompt",
    )(qb, qf, kb, vb, means)


def _diff_lambda(lp_ref):
    lp = lp_ref[...]
    s1 = jnp.sum(lp[0:1, :] * lp[1:2, :], axis=1, keepdims=True)
    s2 = jnp.sum(lp[2:3, :] * lp[3:4, :], axis=1, keepdims=True)
    return jnp.exp(s1) - jnp.exp(s2)


def _diff_finish(o1, o2, lam, subln, lam_init):
    o = o1 - lam * o2
    ms = jnp.mean(o * o, axis=-1, keepdims=True)
    return (o * lax.rsqrt(ms + SUBLN_EPS) * subln) * (1.0 - lam_init)


def _diff_prompt_kernel(qb_ref, k_ref, v_ref, lp_ref, subln_ref, o_ref, *, lam_init):
    blk = MOBA_BLOCK
    nq = qb_ref.shape[0] // blk
    dd = HEAD_DIM // 2
    lam = _diff_lambda(lp_ref) + lam_init
    subln = subln_ref[...]
    lane = lax.broadcasted_iota(jnp.int32, (blk, HEAD_DIM), 1)
    r_id = lax.broadcasted_iota(jnp.int32, (2 * blk, blk), 0) % blk
    causal = lax.broadcasted_iota(jnp.int32, (2 * blk, blk), 1) <= r_id
    for qi in range(nq):
        rows = slice(qi * blk, (qi + 1) * blk)
        q = qb_ref[rows, :].astype(F32)
        q_stack = jnp.concatenate([jnp.where(lane < dd, q, 0.0), jnp.where(lane >= dd, q, 0.0)], axis=0).astype(BF16)
        s_blocks = []
        for kj in range(qi + 1):
            s = _nt_dot(q_stack, k_ref[kj * blk:(kj + 1) * blk, :])
            if kj == qi:
                s = jnp.where(causal, s, NEG_INF)
            s_blocks.append(s)
        o = _softmax_pv(s_blocks, v_ref)
        o_ref[rows, :] = _diff_finish(o[:blk], o[blk:], lam, subln, lam_init).astype(o_ref.dtype)


def _diff_prompt(qb, kb, vb, lp, subln, *, batch, seq, lam_init):
    m, width = qb.shape
    n_heads = width // HEAD_DIM
    spec = pl.BlockSpec((seq, HEAD_DIM), lambda b, h: (b, h))
    return pl.pallas_call(
        functools.partial(_diff_prompt_kernel, lam_init=lam_init),
        grid=(batch, n_heads),
        in_specs=[spec, spec, spec,
                  pl.BlockSpec(lp.shape, lambda b, h: (0, 0)),
                  pl.BlockSpec((1, HEAD_DIM), lambda b, h: (0, 0))],
        out_specs=spec,
        out_shape=jax.ShapeDtypeStruct((m, width), BF16),
        compiler_params=_params("parallel", "parallel"),
        name="diff_prompt",
    )(qb, kb, vb, lp, subln.reshape(1, HEAD_DIM))


def _mem_attn_kernel(q_ref, mk_ref, mv_ref, o_ref):
    tq = q_ref.shape[0]
    q = _pad_rows(q_ref[...], BF16_SUBLANES).astype(BF16)
    s = _nt_dot(q, mk_ref[...].astype(BF16)) * (HEAD_DIM ** -0.5)
    m = jnp.max(s, axis=1, keepdims=True)
    p = jnp.exp(s - m)
    l = jnp.sum(p, axis=1, keepdims=True)
    o = jnp.dot(p.astype(BF16), mv_ref[...].astype(BF16), preferred_element_type=F32)
    o_ref[...] = (o / l)[:tq].astype(o_ref.dtype)


def _mem_attn(proj, mem_k, mem_v, *, batch, seq, q_col_block, kv_col_blocks):
    m = proj.shape[0]
    n_mem = mem_k.shape[1]
    n_h = (proj.shape[1] // HEAD_DIM) - q_col_block
    tq = _pick(seq, (512, 256, 128, 64, 32, 16, 8))
    nq = seq // tq
    k_spec = pl.BlockSpec((None, n_mem, HEAD_DIM), lambda b, i, h: (b, 0, kv_col_blocks[0] + h))
    v_spec = pl.BlockSpec((None, n_mem, HEAD_DIM), lambda b, i, h: (b, 0, kv_col_blocks[1] + h))
    return pl.pallas_call(
        _mem_attn_kernel,
        grid=(batch, nq, n_h),
        in_specs=[pl.BlockSpec((tq, HEAD_DIM), lambda b, i, h: (b * nq + i, q_col_block + h)), k_spec, v_spec],
        out_specs=pl.BlockSpec((tq, HEAD_DIM), lambda b, i, h: (b * nq + i, h)),
        out_shape=jax.ShapeDtypeStruct((m, n_h * HEAD_DIM), BF16 if tq % BF16_SUBLANES == 0 else F32),
        compiler_params=_params("parallel", "parallel", "parallel"),
        name="mem_attn",
    )(proj, mem_k, mem_v)


def _ffn_up_kernel(x_ref, ln_ref, wg_ref, wa_ref, st_ref, cw_ref, cb_ref, act_ref, nst_ref, h_ref, *, seq, chunk):
    @pl.when(pl.program_id(1) == 0)
    def _():
        x = x_ref[...]
        ms = jnp.mean(x * x, axis=-1, keepdims=True)
        h_ref[...] = (x * lax.rsqrt(ms + EPS) * ln_ref[...]).astype(BF16)

    tm = h_ref.shape[0]
    w = cw_ref[...]
    bias = cb_ref[...]
    wg = wg_ref[...].astype(BF16)
    wa = wa_ref[...].astype(BF16)

    def gate(g0, g1, g2, a):
        c = bias + w[0:1, :] * g2 + w[1:2, :] * g1 + w[2:3, :] * g0
        return (c / (1.0 + jnp.exp(-c)) * a).astype(act_ref.dtype)

    carry = None
    for r in range(tm // chunk):
        rows = slice(r * chunk, (r + 1) * chunk)
        h = h_ref[rows, :]
        g = jnp.dot(h, wg, preferred_element_type=F32)
        a = jnp.dot(h, wa, preferred_element_type=F32)
        g1 = pltpu.roll(g, 1, 0)
        g2 = pltpu.roll(g, 2, 0)
        if seq % chunk == 0:
            if (r * chunk) % seq == 0:
                carry = st_ref[(r * chunk) // seq]
            act_ref[rows, :] = gate(g, g1, g2, a)
            top = slice(r * chunk, r * chunk + BF16_SUBLANES)
            row = lax.broadcasted_iota(jnp.int32, (BF16_SUBLANES, g.shape[1]), 0)
            gt = g[:BF16_SUBLANES]
            t1 = jnp.where(row == 0, carry[1:2, :], pltpu.roll(gt, 1, 0))
            t2 = jnp.where(row == 0, carry[0:1, :], jnp.where(row == 1, carry[1:2, :], pltpu.roll(gt, 2, 0)))
            act_ref[top, :] = gate(gt, t1, t2, a[:BF16_SUBLANES])
            carry = g[chunk - (CONV_W - 1):, :]
            if ((r + 1) * chunk) % seq == 0:
                nst_ref[((r + 1) * chunk) // seq - 1] = carry
        else:
            row = lax.broadcasted_iota(jnp.int32, g.shape, 0)
            for b in range(chunk // seq):
                st = st_ref[r * (chunk // seq) + b]
                g1 = jnp.where(row == b * seq, st[1:2, :], g1)
                g2 = jnp.where(row == b * seq, st[0:1, :], jnp.where(row == b * seq + 1, st[1:2, :], g2))
                nst_ref[r * (chunk // seq) + b] = g[(b + 1) * seq - (CONV_W - 1):(b + 1) * seq, :]
            act_ref[rows, :] = gate(g, g1, g2, a)


def _ffn_up(x, ln, w_stack, layer, state, conv_w, conv_b, *, batch, seq):
    m, d = x.shape
    d_ff = w_stack.shape[2] // 2
    assert seq >= CONV_W - 1
    tm = seq * _pick(batch, (max(1, FFN_ROW_TILE // seq),))
    n_seq = tm // seq
    if seq % BF16_SUBLANES == 0:
        chunk = _pick(seq, (FFN_ROW_CHUNK,))
    else:
        chunk = tm
    assert seq % chunk == 0 or chunk % seq == 0
    tn = _pick(d_ff, (256, 128)) if tm > 1024 else _pick(d_ff, (512, 256, 128))
    nc = d_ff // tn
    x_kwargs = dict(pipeline_mode=pl.Buffered(1)) if tm > 1024 else {}
    return pl.pallas_call(
        functools.partial(_ffn_up_kernel, seq=seq, chunk=chunk),
        grid=(m // tm, nc),
        in_specs=[
            pl.BlockSpec((tm, d), lambda i, j: (i, 0), **x_kwargs),
            pl.BlockSpec((1, d), lambda i, j: (0, 0)),
            pl.BlockSpec((None, d, tn), lambda i, j: (layer, 0, j)),
            pl.BlockSpec((None, d, tn), lambda i, j: (layer, 0, nc + j)),
            pl.BlockSpec((n_seq, CONV_W - 1, tn), lambda i, j: (i, 0, j)),
            pl.BlockSpec((CONV_W, tn), lambda i, j: (0, j)),
            pl.BlockSpec((1, tn), lambda i, j: (0, j)),
        ],
        out_specs=[
            pl.BlockSpec((tm, tn), lambda i, j: (i, j)),
            pl.BlockSpec((n_seq, CONV_W - 1, tn), lambda i, j: (i, 0, j)),
        ],
        out_shape=[
            jax.ShapeDtypeStruct((m, d_ff), BF16),
            jax.ShapeDtypeStruct((batch, CONV_W - 1, d_ff), F32),
        ],
        scratch_shapes=[pltpu.VMEM((tm, d), BF16)],
        compiler_params=_params("parallel", "arbitrary"),
        name="ffn_up",
    )(x, ln.reshape(1, d), w_stack, w_stack, state, conv_w, conv_b.reshape(1, d_ff))


def _page_mean_kernel(pt_ref, *refs):
    del pt_ref
    o_ref = refs[-1]
    for g in range(o_ref.shape[0]):
        s = jnp.sum(refs[2 * g][...], axis=0, keepdims=True) + jnp.sum(refs[2 * g + 1][...], axis=0, keepdims=True)
        o_ref[g:g + 1, :] = s * (1.0 / MOBA_BLOCK)


def _page_means(cache_k, page_table_flat, layer, *, batch, n_pages):
    page, width = cache_k.shape[2:]
    assert MOBA_BLOCK == 2 * page
    n_blocks = n_pages // 2
    nbs = _pick(n_blocks, (BLOCKS_PER_STEP, 2, 1))
    n_steps = n_blocks // nbs

    def k_spec(j):
        return pl.BlockSpec((None, None, page, width),
                            lambda b, n, pt: (layer, pt[b * n_pages + 2 * nbs * n + j], 0, 0))

    out = pl.pallas_call(
        _page_mean_kernel,
        grid_spec=pltpu.PrefetchScalarGridSpec(
            num_scalar_prefetch=1,
            grid=(batch, n_steps),
            in_specs=[k_spec(j) for j in range(2 * nbs)],
            out_specs=pl.BlockSpec((None, None, nbs, width), lambda b, n, pt: (b, n, 0, 0)),
        ),
        out_shape=jax.ShapeDtypeStruct((batch, n_steps, nbs, width), F32),
        compiler_params=_params("parallel", "parallel"),
        name="page_means",
    )(page_table_flat, *([cache_k] * (2 * nbs)))
    return out.reshape(batch, n_blocks, width)


def _topk_kernel(q_ref, mean_ref, o_ref):
    n_heads = q_ref.shape[1] // HEAD_DIM
    n_blocks = mean_ref.shape[0]
    for h in range(n_heads):
        sl = slice(h * HEAD_DIM, (h + 1) * HEAD_DIM)
        gate = lax.dot_general(q_ref[:, sl], mean_ref[:, sl], (((1,), (1,)), ((), ())),
                               precision=lax.Precision.HIGHEST, preferred_element_type=F32)
        lane = lax.broadcasted_iota(jnp.int32, gate.shape, 1)
        out_lane = lax.broadcasted_iota(jnp.int32, (gate.shape[0], HEAD_DIM), 1)
        out = jnp.zeros((gate.shape[0], HEAD_DIM), jnp.int32)
        for r in range(MOBA_TOPK):
            mx = jnp.max(gate, axis=1, keepdims=True)
            idx = jnp.min(jnp.where(gate == mx, lane, n_blocks), axis=1, keepdims=True)
            out = jnp.where(out_lane == r, idx, out)
            gate = jnp.where(lane == idx, NEG_INF, gate)
        o_ref[h] = out


def _sample_topk(qf, means, *, batch, seq):
    width = qf.shape[1]
    n_heads = width // HEAD_DIM
    n_blocks = means.shape[1]
    return pl.pallas_call(
        _topk_kernel,
        grid=(batch,),
        in_specs=[pl.BlockSpec((seq, width), lambda b: (b, 0)),
                  pl.BlockSpec((None, n_blocks, width), lambda b: (b, 0, 0))],
        out_specs=pl.BlockSpec((None, n_heads, seq, HEAD_DIM), lambda b: (b, 0, 0, 0)),
        out_shape=jax.ShapeDtypeStruct((batch, n_heads, seq, HEAD_DIM), jnp.int32),
        compiler_params=_params("parallel"),
        name="sample_topk",
    )(qf, means)


def _moba_sample_kernel(phys_ref, q_ref, kn_ref, vn_ref, *refs, nqs):
    del phys_ref
    n_tiles = 2 * MOBA_TOPK
    k_refs = refs[:nqs * n_tiles]
    v_refs = refs[nqs * n_tiles:2 * nqs * n_tiles]
    o_ref = refs[2 * nqs * n_tiles]
    c0 = pl.program_id(2) * nqs
    scale = HEAD_DIM ** -0.5
    seq = q_ref.shape[0]
    k_own = kn_ref[...]
    v_own = vn_ref[...]
    key_id = lax.broadcasted_iota(jnp.int32, (seq, 1), 0)

    @pl.when(c0 == 0)
    def _():
        o_ref[...] = jnp.zeros_like(o_ref)

    out = o_ref[...]
    row = lax.broadcasted_iota(jnp.int32, out.shape, 0)
    for j in range(nqs):
        c = c0 + j
        q_row = q_ref[pl.ds(c, 1), :]
        s_own = jnp.sum(k_own * q_row, axis=1, keepdims=True) * scale
        s_own = jnp.where(key_id <= c, s_own, NEG_INF)
        s_past = [jnp.sum(k_ref[...] * q_row, axis=1, keepdims=True) * scale
                  for k_ref in k_refs[j * n_tiles:(j + 1) * n_tiles]]
        m = jnp.max(s_own, axis=0, keepdims=True)
        for s in s_past:
            m = jnp.maximum(m, jnp.max(s, axis=0, keepdims=True))
        p_own = jnp.exp(s_own - m)
        l = jnp.sum(p_own, axis=0, keepdims=True)
        acc = jnp.sum(p_own * v_own, axis=0, keepdims=True)
        for s, v_ref in zip(s_past, v_refs[j * n_tiles:(j + 1) * n_tiles]):
            p = jnp.exp(s - m)
            l = l + jnp.sum(p, axis=0, keepdims=True)
            acc = acc + jnp.sum(p * v_ref[...], axis=0, keepdims=True)
        out = jnp.where(row == c, acc / l, out)
    o_ref[...] = out


def _moba_sample(qf, k_new, v_new, cache_k, cache_v, phys_flat, layer, *, batch, seq):
    width = qf.shape[1]
    n_heads = width // HEAD_DIM
    page = cache_k.shape[2]
    nqs = _pick(seq, (QUERIES_PER_STEP, 2, 1))
    n_tiles = 2 * MOBA_TOPK

    def kv_spec(j, t):
        def imap(b, h, c, phys):
            return (layer, phys[((b * n_heads + h) * seq + c * nqs + j) * n_tiles + t], 0, h)
        return pl.BlockSpec((None, None, page, HEAD_DIM), imap)

    tiles = [kv_spec(j, t) for j in range(nqs) for t in range(n_tiles)]
    new_spec = pl.BlockSpec((seq, HEAD_DIM), lambda b, h, c, phys: (b, h))
    return pl.pallas_call(
        functools.partial(_moba_sample_kernel, nqs=nqs),
        grid_spec=pltpu.PrefetchScalarGridSpec(
            num_scalar_prefetch=1,
            grid=(batch, n_heads, seq // nqs),
            in_specs=[new_spec, new_spec, new_spec] + tiles + tiles,
            out_specs=new_spec,
        ),
        out_shape=jax.ShapeDtypeStruct((batch * seq, width), F32),
        compiler_params=_params("parallel", "parallel", "arbitrary"),
        name="moba_sample",
    )(phys_flat, qf, k_new, v_new, *([cache_k] * len(tiles)), *([cache_v] * len(tiles)))


def _diff_sample_kernel(pt_ref, q_ref, kn_ref, vn_ref, *refs, n_pg, lam_init):
    del pt_ref
    k_refs = refs[:n_pg]
    v_refs = refs[n_pg:2 * n_pg]
    lp_ref, subln_ref, o_ref, qbd_ref, m_ref, l_ref, acc_ref = refs[2 * n_pg:]
    p_id = pl.program_id(1)
    n_p = pl.num_programs(1)
    seq, width = q_ref.shape
    page = k_refs[0].shape[0]
    dd = HEAD_DIM // 2
    n_sub = width // dd
    n_heads = width // HEAD_DIM
    scale = dd ** -0.5
    g = 2 * seq

    @pl.when(p_id == 0)
    def _():
        q = q_ref[...]
        q_rep = jnp.concatenate([q] * n_sub, axis=0)
        r_sub = lax.broadcasted_iota(jnp.int32, q_rep.shape, 0) // seq
        c_sub = lax.broadcasted_iota(jnp.int32, q_rep.shape, 1) // dd
        qbd_ref[...] = jnp.where(r_sub == c_sub, q_rep, 0.0).astype(BF16)
        m_ref[...] = jnp.full(m_ref.shape, NEG_INF, F32)
        l_ref[...] = jnp.zeros(l_ref.shape, F32)
        acc_ref[...] = jnp.zeros(acc_ref.shape, F32)

    def update(s_list, v_list):
        m_old = m_ref[...]
        m_new = m_old
        for s in s_list:
            m_new = jnp.maximum(m_new, jnp.max(s, axis=1, keepdims=True))
        alpha = jnp.exp(m_old - m_new)
        p_list = [jnp.exp(s - m_new).astype(BF16) for s in s_list]
        l_new = alpha * l_ref[...]
        for p in p_list:
            l_new = l_new + jnp.sum(p.astype(F32), axis=1, keepdims=True)
        l_ref[...] = l_new
        m_ref[...] = m_new
        for h in range(n_heads):
            hr = slice(h * g, (h + 1) * g)
            hc = slice(h * HEAD_DIM, (h + 1) * HEAD_DIM)
            acc = alpha[hr, :] * acc_ref[hr, :]
            for p, v in zip(p_list, v_list):
                acc = acc + jnp.dot(p[hr, :], v[:, hc], preferred_element_type=F32)
            acc_ref[hr, :] = acc

    qbd = qbd_ref[...]
    update([_nt_dot(qbd, k_ref[...].astype(BF16)) * scale for k_ref in k_refs],
           [v_ref[...].astype(BF16) for v_ref in v_refs])

    @pl.when(p_id == n_p - 1)
    def _():
        s = _nt_dot(qbd_ref[...], _pad_rows(kn_ref[...], page).astype(BF16)) * scale
        q_pos = lax.broadcasted_iota(jnp.int32, s.shape, 0) % seq
        k_pos = lax.broadcasted_iota(jnp.int32, s.shape, 1)
        update([jnp.where(k_pos <= q_pos, s, NEG_INF)], [_pad_rows(vn_ref[...], page).astype(BF16)])
        o = acc_ref[...] / l_ref[...]
        lam = _diff_lambda(lp_ref) + lam_init
        subln = subln_ref[...]
        for h in range(n_heads):
            o1 = o[h * g:h * g + seq, :]
            o2 = o[h * g + seq:(h + 1) * g, :]
            o_ref[:, h * HEAD_DIM:(h + 1) * HEAD_DIM] = _diff_finish(o1, o2, lam, subln, lam_init)


def _diff_sample(qf, k_new, v_new, cache_k, cache_v, page_table_flat, lp, subln, layer, *,
                 batch, seq, n_pages, lam_init):
    width = qf.shape[1]
    page = cache_k.shape[2]
    rows = (width // (HEAD_DIM // 2)) * seq
    n_pg = _pick(n_pages, (PAGES_PER_STEP, 2, 1))
    new_spec = pl.BlockSpec((seq, width), lambda b, p, pt: (b, 0))

    def kv_spec(j):
        return pl.BlockSpec((None, None, page, width),
                            lambda b, p, pt: (layer, pt[b * n_pages + n_pg * p + j], 0, 0))

    kv_specs = [kv_spec(j) for j in range(n_pg)]
    return pl.pallas_call(
        functools.partial(_diff_sample_kernel, n_pg=n_pg, lam_init=lam_init),
        grid_spec=pltpu.PrefetchScalarGridSpec(
            num_scalar_prefetch=1,
            grid=(batch, n_pages // n_pg),
            in_specs=[new_spec, new_spec, new_spec] + kv_specs + kv_specs +
                     [pl.BlockSpec(lp.shape, lambda b, p, pt: (0, 0)),
                      pl.BlockSpec((1, HEAD_DIM), lambda b, p, pt: (0, 0))],
            out_specs=new_spec,
            scratch_shapes=[pltpu.VMEM((rows, width), BF16), pltpu.VMEM((rows, 1), F32),
                            pltpu.VMEM((rows, 1), F32), pltpu.VMEM((rows, HEAD_DIM), F32)],
        ),
        out_shape=jax.ShapeDtypeStruct((batch * seq, width), F32),
        compiler_params=_params("parallel", "arbitrary"),
        name="diff_sample",
    )(page_table_flat, qf, k_new, v_new, *([cache_k] * n_pg), *([cache_v] * n_pg), lp, subln.reshape(1, HEAD_DIM))


def _layer(i, x, pos, mem_k, mem_v, mem_cols, conv_state, wts, *, batch, seq, paged=None, stack=None):
    width = wts["self_width"]
    is_moba = i % N_MIXERS == 0
    prompt = paged is None
    head_dim = HEAD_DIM if is_moba else HEAD_DIM // 2
    proj = _norm_matmul(x, wts["ln_attn"][i], wts["w_in"], i)
    cos_t, sin_t = _rope_tables(pos, head_dim)
    r = _rope(proj, cos_t, sin_t, seq=seq, width=width, head_dim=head_dim, with_qf=is_moba or not prompt,
              with_bf16=prompt, with_means=is_moba and prompt, stack=stack)
    if is_moba:
        if prompt:
            means = r["means"].reshape(batch, seq // MOBA_BLOCK, width)
            self_out = _moba_prompt(r["qb"], r["q"], r["kb"], r["vb"], means, batch=batch, seq=seq)
        else:
            cache_k, cache_v, pt_flat, n_pages = paged
            means = _page_means(cache_k, pt_flat, i, batch=batch, n_pages=n_pages)
            assert n_pages // 2 >= MOBA_TOPK
            idx = _sample_topk(r["q"], means, batch=batch, seq=seq)[..., :MOBA_TOPK]
            logical = (2 * idx[..., None] + jnp.arange(2, dtype=jnp.int32)).reshape(batch, -1)
            phys = jnp.take_along_axis(pt_flat.reshape(batch, n_pages), logical, axis=1)
            self_out = _moba_sample(r["q"], r["k"], r["v"], cache_k, cache_v, phys.reshape(-1), i,
                                    batch=batch, seq=seq).astype(BF16)
    else:
        j = i // N_MIXERS
        lam_init = 0.8 - 0.6 * math.exp(-0.3 * i)
        lp = wts["diff_lambda"][j]
        subln = wts["diff_subln"][j]
        if prompt:
            self_out = _diff_prompt(r["qb"], r["kb"], r["vb"], lp, subln, batch=batch, seq=seq, lam_init=lam_init)
        else:
            cache_k, cache_v, pt_flat, n_pages = paged
            self_out = _diff_sample(r["q"], r["k"], r["v"], cache_k, cache_v, pt_flat, lp, subln, i,
                                    batch=batch, seq=seq, n_pages=n_pages, lam_init=lam_init).astype(BF16)
    mem_out = _mem_attn(proj, mem_k, mem_v, batch=batch, seq=seq, q_col_block=3 * width // HEAD_DIM,
                        kv_col_blocks=mem_cols)
    x = _matmul_res([self_out, mem_out.astype(BF16)], wts["w_o"], i, x)
    act, new_conv = _ffn_up(x, wts["ln_ffn"][i], wts["w_up"], i, conv_state, wts["conv_w"][i], wts["conv_b"][i],
                            batch=batch, seq=seq)
    x = _matmul_res([act], wts["w_down"], i, x, tm_prefs=(1024, 512, 256, 128, 64),
                    tn_prefs=(256, 128))
    return x, r["k"], r["v"], new_conv


def kernel(x_prompt, x_sample, mem_prompt, cache_k, cache_v, cache_mem_k, cache_mem_v, state_conv, page_table,
           ln_attn, w_in, w_o, diff_lambda, diff_subln, ln_mem, w_mem_kv, ln_ffn, w_up, conv_w, conv_b,
           w_down, ln_final):
    bp, tp, d_model = x_prompt.shape
    bs, ts, _ = x_sample.shape
    depth = w_in.shape[0]
    n_mem = mem_prompt.shape[1]
    mem_width = w_mem_kv.shape[2] // 2
    self_width = d_model - mem_width
    d_ff = w_down.shape[1]
    n_pages = page_table.shape[1]
    page = cache_k.shape[2]
    past = n_pages * page
    assert tp % MOBA_BLOCK == 0 and past % MOBA_BLOCK == 0 and ts <= MOBA_BLOCK

    wts = dict(self_width=self_width, ln_attn=ln_attn, ln_ffn=ln_ffn, diff_lambda=diff_lambda,
               diff_subln=diff_subln, conv_w=conv_w, conv_b=conv_b, w_in=w_in, w_o=w_o, w_up=w_up, w_down=w_down)
    pos_p = jnp.arange(tp, dtype=jnp.int32)
    pos_s = past + jnp.arange(ts, dtype=jnp.int32)
    conv0 = jnp.zeros((bp, CONV_W - 1, d_ff), F32)
    pt_flat = page_table.reshape(-1).astype(jnp.int32)
    paged = (cache_k, cache_v, pt_flat, n_pages)
    mem_heads = mem_width // HEAD_DIM

    xp = x_prompt.reshape(bp * tp, d_model)
    xs = x_sample.reshape(bs * ts, d_model)
    mem_rows = mem_prompt.reshape(bp * n_mem, d_model)
    k_stack = v_stack = None
    mkv_l, cp_l, ks_l, vs_l, cs_l = [], [], [], [], []
    for i in range(depth):
        mem_kv = _norm_matmul(mem_rows, ln_mem[i], w_mem_kv, i).reshape(bp, n_mem, 2 * mem_width)
        xp, k_stack, v_stack, cp = _layer(i, xp, pos_p, mem_kv, mem_kv, (0, mem_heads), conv0, wts,
                                          batch=bp, seq=tp, stack=(i, depth, k_stack, v_stack))
        xs, k_s, v_s, c_s = _layer(i, xs, pos_s, cache_mem_k[i], cache_mem_v[i], (0, 0), state_conv[i], wts,
                                   batch=bs, seq=ts, paged=paged)
        mkv_l.append(mem_kv)
        cp_l.append(cp)
        ks_l.append(k_s.reshape(bs, ts, self_width))
        vs_l.append(v_s.reshape(bs, ts, self_width))
        cs_l.append(c_s)
    y_prompt = _rmsnorm(xp, ln_final).reshape(bp, tp, d_model)
    y_sample = _rmsnorm(xs, ln_final).reshape(bs, ts, d_model)
    mkv = jnp.stack(mkv_l)
    return (y_prompt, y_sample,
            k_stack.reshape(depth, bp, tp, self_width), v_stack.reshape(depth, bp, tp, self_width),
            mkv[..., :mem_width], mkv[..., mem_width:], jnp.stack(cp_l),
            jnp.stack(ks_l), jnp.stack(vs_l), jnp.stack(cs_l))
```

```python
import functools
import math

import jax
import jax.numpy as jnp
from jax import lax
from jax.experimental import pallas as pl
from jax.experimental.pallas import tpu as pltpu

F32 = jnp.float32
BF16 = jnp.bfloat16

HEAD_DIM = 128
MOBA_BLOCK = 256
MOBA_TOPK = 3
CONV_W = 3
ROPE_THETA = 10000.0
EPS = 1e-6
SUBLN_EPS = 1e-5
N_MIXERS = 2

VMEM_LIMIT_BYTES = 54 * 1024 * 1024
NEG_INF = float("-inf")
LOG2_E = 1.4426950408889634
BF16_SUBLANES = 16
FFN_ROW_TILE = 2048
FFN_ROW_CHUNK = 256
PAGES_PER_STEP = 8
BLOCKS_PER_STEP = 4


def _params(*sem):
    return pltpu.CompilerParams(dimension_semantics=sem, vmem_limit_bytes=VMEM_LIMIT_BYTES)


def _pick(n, prefs):
    for p in prefs:
        if n % p == 0:
            return p
    return n


def _nt_dot(a, b):
    return lax.dot_general(a, b, (((1,), (1,)), ((), ())), preferred_element_type=F32)


def _pad_rows(x, rows):
    if x.shape[0] >= rows:
        return x
    return jnp.concatenate([x, jnp.zeros((rows - x.shape[0],) + x.shape[1:], x.dtype)], axis=0)


def _norm_matmul_kernel(x_ref, g_ref, w_ref, o_ref, h_ref):
    @pl.when(pl.program_id(1) == 0)
    def _():
        x = x_ref[...]
        ms = jnp.mean(x * x, axis=-1, keepdims=True)
        h_ref[...] = (x * lax.rsqrt(ms + EPS) * g_ref[...]).astype(BF16)

    o_ref[...] = jnp.dot(h_ref[...], w_ref[...].astype(BF16), preferred_element_type=F32)


def _norm_matmul(x, g, w_stack, layer):
    m, d = x.shape
    n = w_stack.shape[2]
    tm = _pick(m, (2048, 1024, 512, 256, 128, 64))
    tn = _pick(n, (512, 256, 128))
    x_kwargs = dict(pipeline_mode=pl.Buffered(1)) if tm > 1024 else {}
    return pl.pallas_call(
        _norm_matmul_kernel,
        grid=(m // tm, n // tn),
        in_specs=[
            pl.BlockSpec((tm, d), lambda i, j: (i, 0), **x_kwargs),
            pl.BlockSpec((1, d), lambda i, j: (0, 0)),
            pl.BlockSpec((None, d, tn), lambda i, j: (layer, 0, j)),
        ],
        out_specs=pl.BlockSpec((tm, tn), lambda i, j: (i, j)),
        out_shape=jax.ShapeDtypeStruct((m, n), F32),
        scratch_shapes=[pltpu.VMEM((tm, d), BF16)],
        compiler_params=_params("parallel", "arbitrary"),
        name="norm_matmul",
    )(x, g.reshape(1, d), w_stack)


def _matmul_res_kernel(*refs, n_parts):
    a_refs = refs[:n_parts]
    w_refs = refs[n_parts:2 * n_parts]
    r_ref, o_ref = refs[2 * n_parts:]
    acc = r_ref[...]
    for a_ref, w_ref in zip(a_refs, w_refs):
        acc = acc + jnp.dot(a_ref[...], w_ref[...].astype(BF16), preferred_element_type=F32)
    o_ref[...] = acc


def _matmul_res(a_parts, w_stack, layer, res, *, tm_prefs=(2048, 1024, 512, 256, 128, 64), tn_prefs=(512, 256, 128)):
    m, n = res.shape
    tm = _pick(m, tm_prefs)
    tn = _pick(n, tn_prefs)
    a_kwargs = dict(pipeline_mode=pl.Buffered(1)) if tm > 1024 else {}
    a_specs, w_specs = [], []
    row0 = 0
    for a in a_parts:
        k = a.shape[1]
        assert row0 % k == 0
        a_specs.append(pl.BlockSpec((tm, k), lambda i, j: (i, 0), **a_kwargs))
        w_specs.append(pl.BlockSpec((None, k, tn), lambda i, j, rb=row0 // k: (layer, rb, j)))
        row0 += k
    assert row0 == w_stack.shape[1]
    io_spec = pl.BlockSpec((tm, tn), lambda i, j: (i, j))
    return pl.pallas_call(
        functools.partial(_matmul_res_kernel, n_parts=len(a_parts)),
        grid=(m // tm, n // tn),
        in_specs=a_specs + w_specs + [io_spec],
        out_specs=io_spec,
        out_shape=jax.ShapeDtypeStruct((m, n), F32),
        compiler_params=_params("parallel", "parallel"),
        name="matmul_res",
    )(*a_parts, *([w_stack] * len(a_parts)), res)


def _rmsnorm_kernel(x_ref, g_ref, o_ref):
    x = x_ref[...]
    ms = jnp.mean(x * x, axis=-1, keepdims=True)
    o_ref[...] = x * lax.rsqrt(ms + EPS) * g_ref[...]


def _rmsnorm(x, g):
    m, d = x.shape
    tm = _pick(m, (512, 256, 128, 64))
    return pl.pallas_call(
        _rmsnorm_kernel,
        grid=(m // tm,),
        in_specs=[pl.BlockSpec((tm, d), lambda i: (i, 0)), pl.BlockSpec((1, d), lambda i: (0, 0))],
        out_specs=pl.BlockSpec((tm, d), lambda i: (i, 0)),
        out_shape=jax.ShapeDtypeStruct((m, d), F32),
        compiler_params=_params("parallel"),
        name="rmsnorm",
    )(x, g.reshape(1, d))


def _rope_tables(pos, head_dim):
    half = head_dim // 2
    inv = jnp.exp(-math.log(ROPE_THETA) * jnp.arange(half, dtype=F32) / half)
    ang = pos.astype(F32)[:, None] * inv[None, :]
    cos, sin = jnp.cos(ang), jnp.sin(ang)
    reps = HEAD_DIM // head_dim
    cos_t = jnp.tile(jnp.concatenate([cos, cos], axis=-1), (1, reps))
    sin_t = jnp.tile(jnp.concatenate([-sin, sin], axis=-1), (1, reps))
    return cos_t, sin_t


def _rope_kernel(*refs, head_dim, n_stack_in, with_qf, with_bf16, with_means):
    qb_scale = head_dim ** -0.5 * LOG2_E
    refs = refs[n_stack_in:]
    q_ref, k_ref, v_ref, cos_ref, sin_ref = refs[:5]
    outs = list(refs[5:])
    kf_ref, vf_ref = outs[:2]
    outs = outs[2:]
    qf_ref = outs.pop(0) if with_qf else None
    if with_bf16:
        qb_ref, kb_ref, vb_ref = outs[:3]
        outs = outs[3:]
    mean_ref = outs.pop(0) if with_means else None
    cos = cos_ref[...]
    sin = sin_ref[...]
    half = head_dim // 2
    n_groups = q_ref.shape[1] // HEAD_DIM
    if head_dim != HEAD_DIM:
        lane = lax.broadcasted_iota(jnp.int32, cos.shape, 1)
        first_half = (lane % head_dim) < half

    def rot(x):
        if head_dim == HEAD_DIM:
            partner = pltpu.roll(x, half, 1)
        else:
            partner = jnp.where(first_half, pltpu.roll(x, HEAD_DIM - half, 1), pltpu.roll(x, half, 1))
        return x * cos + partner * sin

    for h in range(n_groups):
        sl = slice(h * HEAD_DIM, (h + 1) * HEAD_DIM)
        qr = rot(q_ref[:, sl])
        kr = rot(k_ref[:, sl])
        kf_ref[:, sl] = kr
        if with_qf:
            qf_ref[:, sl] = qr
        if with_bf16:
            qb_ref[:, sl] = (qr * qb_scale).astype(BF16)
            kb_ref[:, sl] = kr.astype(BF16)
        if with_means:
            mean_ref[:, sl] = jnp.sum(kr, axis=0, keepdims=True) * (1.0 / MOBA_BLOCK)
    v = v_ref[...]
    vf_ref[...] = v
    if with_bf16:
        vb_ref[...] = v.astype(BF16)


def _rope(proj, cos_t, sin_t, *, seq, width, head_dim, with_qf, with_bf16, with_means, stack=None):
    m = proj.shape[0]
    tr = MOBA_BLOCK if seq % MOBA_BLOCK == 0 else seq
    n_t = seq // tr
    if with_means:
        assert tr == MOBA_BLOCK
    row = lambda c: pl.BlockSpec((tr, width), lambda r: (r, c))
    tab = pl.BlockSpec((tr, HEAD_DIM), lambda r: (r % n_t, 0))
    in_specs = [row(0), row(1), row(2), tab, tab]
    args = [proj, proj, proj, cos_t, sin_t]
    aliases = {}
    if stack is None:
        names = ["k", "v"]
        out_shapes = [jax.ShapeDtypeStruct((m, width), F32)] * 2
        out_specs = [row(0)] * 2
        n_stack_in = 0
    else:
        layer, k_stack, v_stack = stack
        names = ["k", "v"]
        out_shapes = [jax.ShapeDtypeStruct(k_stack.shape, F32)] * 2
        out_specs = [pl.BlockSpec((None, tr, width), lambda r: (layer, r, 0))] * 2
        n_stack_in = 2
        in_specs = [pl.BlockSpec(memory_space=pl.ANY)] * 2 + in_specs
        args = [k_stack, v_stack] + args
        aliases = {0: 0, 1: 1}
    if with_qf:
        names.append("q")
        out_shapes.append(jax.ShapeDtypeStruct((m, width), F32))
        out_specs.append(row(0))
    if with_bf16:
        names += ["qb", "kb", "vb"]
        out_shapes += [jax.ShapeDtypeStruct((m, width), BF16)] * 3
        out_specs += [row(0)] * 3
    if with_means:
        names.append("means")
        out_shapes.append(jax.ShapeDtypeStruct((m // tr, 1, width), F32))
        out_specs.append(pl.BlockSpec((None, 1, width), lambda r: (r, 0, 0)))
    outs = pl.pallas_call(
        functools.partial(_rope_kernel, head_dim=head_dim, n_stack_in=n_stack_in, with_qf=with_qf,
                          with_bf16=with_bf16, with_means=with_means),
        grid=(m // tr,),
        in_specs=in_specs,
        out_specs=out_specs,
        out_shape=out_shapes,
        input_output_aliases=aliases,
        compiler_params=_params("parallel"),
        name="rope",
    )(*args)
    return dict(zip(names, outs))


def _softmax_pv(s_blocks, v_ref):
    blk = MOBA_BLOCK
    lanes = HEAD_DIM
    tiles = [s[:, t * lanes:(t + 1) * lanes] for s in s_blocks for t in range(blk // lanes)]
    m_part = tiles[0]
    for t in tiles[1:]:
        m_part = jnp.maximum(m_part, t)
    m = jnp.max(m_part, axis=1, keepdims=True)
    l_part = None
    acc = None
    for kj, s in enumerate(s_blocks):
        p = jnp.exp2(s - m)
        for t in range(blk // lanes):
            pt = p[:, t * lanes:(t + 1) * lanes]
            l_part = pt if l_part is None else l_part + pt
        pv = jnp.dot(p.astype(BF16), v_ref[kj * blk:(kj + 1) * blk, :], preferred_element_type=F32)
        acc = pv if acc is None else acc + pv
    return acc / jnp.sum(l_part, axis=1, keepdims=True)


def _moba_prompt_kernel(qb_ref, qf_ref, k_ref, v_ref, mean_ref, o_ref):
    blk = MOBA_BLOCK
    nq = qb_ref.shape[0] // blk
    means = mean_ref[...]
    causal = lax.broadcasted_iota(jnp.int32, (blk, blk), 1) <= lax.broadcasted_iota(jnp.int32, (blk, blk), 0)
    for qi in range(nq):
        rows = slice(qi * blk, (qi + 1) * blk)
        q = qb_ref[rows, :]
        sel_bias = None
        if qi > MOBA_TOPK:
            gate = lax.dot_general(qf_ref[rows, :], means, (((1,), (1,)), ((), ())),
                                   precision=lax.Precision.HIGHEST, preferred_element_type=F32)
            blk_id = lax.broadcasted_iota(jnp.int32, gate.shape, 1)
            rank = jnp.zeros(gate.shape, F32)
            for m_id in range(qi):
                gm = gate[:, m_id:m_id + 1]
                beats = (gm > gate) | ((gm == gate) & (blk_id > m_id))
                rank = rank + jnp.where(beats, 1.0, 0.0)
            sel_bias = jnp.where(rank < float(MOBA_TOPK), 0.0, NEG_INF)
        s_blocks = []
        for kj in range(qi + 1):
            s = _nt_dot(q, k_ref[kj * blk:(kj + 1) * blk, :])
            if kj == qi:
                s = jnp.where(causal, s, NEG_INF)
            elif sel_bias is not None:
                s = s + sel_bias[:, kj:kj + 1]
            s_blocks.append(s)
        o_ref[rows, :] = _softmax_pv(s_blocks, v_ref).astype(o_ref.dtype)


def _moba_prompt(qb, qf, kb, vb, means, *, batch, seq):
    m, width = qb.shape
    n_heads = width // HEAD_DIM
    nb = means.shape[1]
    spec = pl.BlockSpec((seq, HEAD_DIM), lambda b, h: (b, h))
    return pl.pallas_call(
        _moba_prompt_kernel,
        grid=(batch, n_heads),
        in_specs=[spec, spec, spec, spec, pl.BlockSpec((None, nb, HEAD_DIM), lambda b, h: (b, 0, h))],
        out_specs=spec,
        out_shape=jax.ShapeDtypeStruct((m, width), BF16),
        compiler_params=_params("parallel", "parallel"),
        name="moba_prompt",
    )(qb, qf, kb, vb, means)


def _diff_lambda(lp_ref):
    lp = lp_ref[...]
    s1 = jnp.sum(lp[0:1, :] * lp[1:2, :], axis=1, keepdims=True)
    s2 = jnp.sum(lp[2:3, :] * lp[3:4, :], axis=1, keepdims=True)
    return jnp.exp(s1) - jnp.exp(s2)


def _diff_finish(o1, o2, lam, subln, lam_init):
    o = o1 - lam * o2
    ms = jnp.mean(o * o, axis=-1, keepdims=True)
    return (o * lax.rsqrt(ms + SUBLN_EPS) * subln) * (1.0 - lam_init)


def _diff_prompt_kernel(qb_ref, k_ref, v_ref, lp_ref, subln_ref, o_ref, *, lam_init):
    blk = MOBA_BLOCK
    nq = qb_ref.shape[0] // blk
    dd = HEAD_DIM // 2
    lam = _diff_lambda(lp_ref) + lam_init
    subln = subln_ref[...]
    lane = lax.broadcasted_iota(jnp.int32, (blk, HEAD_DIM), 1)
    r_id = lax.broadcasted_iota(jnp.int32, (2 * blk, blk), 0) % blk
    causal = lax.broadcasted_iota(jnp.int32, (2 * blk, blk), 1) <= r_id
    for qi in range(nq):
        rows = slice(qi * blk, (qi + 1) * blk)
        q = qb_ref[rows, :].astype(F32)
        q_stack = jnp.concatenate([jnp.where(lane < dd, q, 0.0), jnp.where(lane >= dd, q, 0.0)], axis=0).astype(BF16)
        s_blocks = []
        for kj in range(qi + 1):
            s = _nt_dot(q_stack, k_ref[kj * blk:(kj + 1) * blk, :])
            if kj == qi:
                s = jnp.where(causal, s, NEG_INF)
            s_blocks.append(s)
        o = _softmax_pv(s_blocks, v_ref)
        o_ref[rows, :] = _diff_finish(o[:blk], o[blk:], lam, subln, lam_init).astype(o_ref.dtype)


def _diff_prompt(qb, kb, vb, lp, subln, *, batch, seq, lam_init):
    m, width = qb.shape
    n_heads = width // HEAD_DIM
    spec = pl.BlockSpec((seq, HEAD_DIM), lambda b, h: (b, h))
    return pl.pallas_call(
        functools.partial(_diff_prompt_kernel, lam_init=lam_init),
        grid=(batch, n_heads),
        in_specs=[spec, spec, spec,
                  pl.BlockSpec(lp.shape, lambda b, h: (0, 0)),
                  pl.BlockSpec((1, HEAD_DIM), lambda b, h: (0, 0))],
        out_specs=spec,
        out_shape=jax.ShapeDtypeStruct((m, width), BF16),
        compiler_params=_params("parallel", "parallel"),
        name="diff_prompt",
    )(qb, kb, vb, lp, subln.reshape(1, HEAD_DIM))


def _mem_attn_kernel(q_ref, mk_ref, mv_ref, o_ref):
    tq = q_ref.shape[0]
    q = _pad_rows(q_ref[...], BF16_SUBLANES).astype(BF16)
    s = _nt_dot(q, mk_ref[...].astype(BF16)) * (HEAD_DIM ** -0.5)
    m = jnp.max(s, axis=1, keepdims=True)
    p = jnp.exp(s - m)
    l = jnp.sum(p, axis=1, keepdims=True)
    o = jnp.dot(p.astype(BF16), mv_ref[...].astype(BF16), preferred_element_type=F32)
    o_ref[...] = (o / l)[:tq].astype(o_ref.dtype)


def _mem_attn(proj, mem_k, mem_v, *, batch, seq, q_col_block, kv_col_blocks):
    m = proj.shape[0]
    n_mem = mem_k.shape[1]
    n_h = (proj.shape[1] // HEAD_DIM) - q_col_block
    tq = _pick(seq, (2048, 1024, 512, 256, 128, 64, 32, 16, 8))
    nq = seq // tq
    k_spec = pl.BlockSpec((None, n_mem, HEAD_DIM), lambda b, i, h: (b, 0, kv_col_blocks[0] + h))
    v_spec = pl.BlockSpec((None, n_mem, HEAD_DIM), lambda b, i, h: (b, 0, kv_col_blocks[1] + h))
    return pl.pallas_call(
        _mem_attn_kernel,
        grid=(batch, nq, n_h),
        in_specs=[pl.BlockSpec((tq, HEAD_DIM), lambda b, i, h: (b * nq + i, q_col_block + h)), k_spec, v_spec],
        out_specs=pl.BlockSpec((tq, HEAD_DIM), lambda b, i, h: (b * nq + i, h)),
        out_shape=jax.ShapeDtypeStruct((m, n_h * HEAD_DIM), BF16 if tq % BF16_SUBLANES == 0 else F32),
        compiler_params=_params("parallel", "parallel", "parallel"),
        name="mem_attn",
    )(proj, mem_k, mem_v)


def _ffn_up_kernel(x_ref, ln_ref, wg_ref, wa_ref, st_ref, cw_ref, cb_ref, act_ref, nst_ref, h_ref, *, seq, chunk):
    @pl.when(pl.program_id(1) == 0)
    def _():
        x = x_ref[...]
        ms = jnp.mean(x * x, axis=-1, keepdims=True)
        h_ref[...] = (x * lax.rsqrt(ms + EPS) * ln_ref[...]).astype(BF16)

    tm = h_ref.shape[0]
    w = cw_ref[...]
    bias = cb_ref[...]
    wg = wg_ref[...].astype(BF16)
    wa = wa_ref[...].astype(BF16)

    def gate(g0, g1, g2, a):
        c = bias + w[0:1, :] * g2 + w[1:2, :] * g1 + w[2:3, :] * g0
        return (c / (1.0 + jnp.exp(-c)) * a).astype(act_ref.dtype)

    carry = None
    for r in range(tm // chunk):
        rows = slice(r * chunk, (r + 1) * chunk)
        h = h_ref[rows, :]
        g = jnp.dot(h, wg, preferred_element_type=F32)
        a = jnp.dot(h, wa, preferred_element_type=F32)
        g1 = pltpu.roll(g, 1, 0)
        g2 = pltpu.roll(g, 2, 0)
        if seq % chunk == 0:
            if (r * chunk) % seq == 0:
                carry = st_ref[(r * chunk) // seq]
            act_ref[rows, :] = gate(g, g1, g2, a)
            top = slice(r * chunk, r * chunk + BF16_SUBLANES)
            row = lax.broadcasted_iota(jnp.int32, (BF16_SUBLANES, g.shape[1]), 0)
            gt = g[:BF16_SUBLANES]
            t1 = jnp.where(row == 0, carry[1:2, :], pltpu.roll(gt, 1, 0))
            t2 = jnp.where(row == 0, carry[0:1, :], jnp.where(row == 1, carry[1:2, :], pltpu.roll(gt, 2, 0)))
            act_ref[top, :] = gate(gt, t1, t2, a[:BF16_SUBLANES])
            carry = g[chunk - (CONV_W - 1):, :]
            if ((r + 1) * chunk) % seq == 0:
                nst_ref[((r + 1) * chunk) // seq - 1] = carry
        else:
            row = lax.broadcasted_iota(jnp.int32, g.shape, 0)
            for b in range(chunk // seq):
                st = st_ref[r * (chunk // seq) + b]
                g1 = jnp.where(row == b * seq, st[1:2, :], g1)
                g2 = jnp.where(row == b * seq, st[0:1, :], jnp.where(row == b * seq + 1, st[1:2, :], g2))
                nst_ref[r * (chunk // seq) + b] = g[(b + 1) * seq - (CONV_W - 1):(b + 1) * seq, :]
            act_ref[rows, :] = gate(g, g1, g2, a)


def _ffn_up(x, ln, w_stack, layer, state, conv_w, conv_b, *, batch, seq):
    m, d = x.shape
    d_ff = w_stack.shape[2] // 2
    assert seq >= CONV_W - 1
    tm = seq * _pick(batch, (max(1, FFN_ROW_TILE // seq),))
    n_seq = tm // seq
    if seq % BF16_SUBLANES == 0:
        chunk = _pick(seq, (FFN_ROW_CHUNK,))
    else:
        chunk = tm
    assert seq % chunk == 0 or chunk % seq == 0
    tn = _pick(d_ff, (256, 128)) if tm > 1024 else _pick(d_ff, (512, 256, 128))
    nc = d_ff // tn
    x_kwargs = dict(pipeline_mode=pl.Buffered(1)) if tm > 1024 else {}
    return pl.pallas_call(
        functools.partial(_ffn_up_kernel, seq=seq, chunk=chunk),
        grid=(m // tm, nc),
        in_specs=[
            pl.BlockSpec((tm, d), lambda i, j: (i, 0), **x_kwargs),
            pl.BlockSpec((1, d), lambda i, j: (0, 0)),
            pl.BlockSpec((None, d, tn), lambda i, j: (layer, 0, j)),
            pl.BlockSpec((None, d, tn), lambda i, j: (layer, 0, nc + j)),
            pl.BlockSpec((n_seq, CONV_W - 1, tn), lambda i, j: (i, 0, j)),
            pl.BlockSpec((CONV_W, tn), lambda i, j: (0, j)),
            pl.BlockSpec((1, tn), lambda i, j: (0, j)),
        ],
        out_specs=[
            pl.BlockSpec((tm, tn), lambda i, j: (i, j)),
            pl.BlockSpec((n_seq, CONV_W - 1, tn), lambda i, j: (i, 0, j)),
        ],
        out_shape=[
            jax.ShapeDtypeStruct((m, d_ff), BF16),
            jax.ShapeDtypeStruct((batch, CONV_W - 1, d_ff), F32),
        ],
        scratch_shapes=[pltpu.VMEM((tm, d), BF16)],
        compiler_params=_params("parallel", "arbitrary"),
        name="ffn_up",
    )(x, ln.reshape(1, d), w_stack, w_stack, state, conv_w, conv_b.reshape(1, d_ff))


def _page_mean_kernel(pt_ref, *refs):
    del pt_ref
    o_ref = refs[-1]
    for g in range(o_ref.shape[0]):
        s = jnp.sum(refs[2 * g][...], axis=0, keepdims=True) + jnp.sum(refs[2 * g + 1][...], axis=0, keepdims=True)
        o_ref[g:g + 1, :] = s * (1.0 / MOBA_BLOCK)


def _page_means(cache_k, page_table_flat, layer, *, batch, n_pages):
    page, width = cache_k.shape[2:]
    assert MOBA_BLOCK == 2 * page
    n_blocks = n_pages // 2
    nbs = _pick(n_blocks, (BLOCKS_PER_STEP, 2, 1))
    n_steps = n_blocks // nbs

    def k_spec(j):
        return pl.BlockSpec((None, None, page, width),
                            lambda b, n, pt: (layer, pt[b * n_pages + 2 * nbs * n + j], 0, 0))

    out = pl.pallas_call(
        _page_mean_kernel,
        grid_spec=pltpu.PrefetchScalarGridSpec(
            num_scalar_prefetch=1,
            grid=(batch, n_steps),
            in_specs=[k_spec(j) for j in range(2 * nbs)],
            out_specs=pl.BlockSpec((None, None, nbs, width), lambda b, n, pt: (b, n, 0, 0)),
        ),
        out_shape=jax.ShapeDtypeStruct((batch, n_steps, nbs, width), F32),
        compiler_params=_params("parallel", "parallel"),
        name="page_means",
    )(page_table_flat, *([cache_k] * (2 * nbs)))
    return out.reshape(batch, n_blocks, width)


def _topk_kernel(q_ref, mean_ref, o_ref):
    n_heads = q_ref.shape[1] // HEAD_DIM
    n_blocks = mean_ref.shape[0]
    for h in range(n_heads):
        sl = slice(h * HEAD_DIM, (h + 1) * HEAD_DIM)
        gate = lax.dot_general(q_ref[:, sl], mean_ref[:, sl], (((1,), (1,)), ((), ())),
                               precision=lax.Precision.HIGHEST, preferred_element_type=F32)
        lane = lax.broadcasted_iota(jnp.int32, gate.shape, 1)
        out_lane = lax.broadcasted_iota(jnp.int32, (gate.shape[0], HEAD_DIM), 1)
        out = jnp.zeros((gate.shape[0], HEAD_DIM), jnp.int32)
        for r in range(MOBA_TOPK):
            mx = jnp.max(gate, axis=1, keepdims=True)
            idx = jnp.min(jnp.where(gate == mx, lane, n_blocks), axis=1, keepdims=True)
            out = jnp.where(out_lane == r, idx, out)
            gate = jnp.where(lane == idx, NEG_INF, gate)
        o_ref[h] = out


def _sample_topk(qf, means, *, batch, seq):
    width = qf.shape[1]
    n_heads = width // HEAD_DIM
    n_blocks = means.shape[1]
    return pl.pallas_call(
        _topk_kernel,
        grid=(batch,),
        in_specs=[pl.BlockSpec((seq, width), lambda b: (b, 0)),
                  pl.BlockSpec((None, n_blocks, width), lambda b: (b, 0, 0))],
        out_specs=pl.BlockSpec((None, n_heads, seq, HEAD_DIM), lambda b: (b, 0, 0, 0)),
        out_shape=jax.ShapeDtypeStruct((batch, n_heads, seq, HEAD_DIM), jnp.int32),
        compiler_params=_params("parallel"),
        name="sample_topk",
    )(qf, means)


def _moba_sample_kernel(phys_ref, q_ref, kn_ref, vn_ref, ck_hbm, cv_hbm, o_ref, kbuf, vbuf, sem, *, layer, n_heads):
    n = pl.program_id(0)
    n_steps = pl.num_programs(0)
    seq = q_ref.shape[0]
    n_tiles = 2 * MOBA_TOPK
    per_step = seq * n_tiles
    scale = HEAD_DIM ** -0.5
    slot = n % 2

    def tile_copies(step, buf_slot, i):
        page_id = phys_ref[step * per_step + i]
        lane0 = pl.multiple_of((step % n_heads) * HEAD_DIM, HEAD_DIM)
        kc = pltpu.make_async_copy(ck_hbm.at[layer, page_id, :, pl.ds(lane0, HEAD_DIM)], kbuf.at[buf_slot, i],
                                   sem.at[buf_slot, 0])
        vc = pltpu.make_async_copy(cv_hbm.at[layer, page_id, :, pl.ds(lane0, HEAD_DIM)], vbuf.at[buf_slot, i],
                                   sem.at[buf_slot, 1])
        return kc, vc

    def start_tile(step, buf_slot, i):
        for c in tile_copies(step, buf_slot, i):
            c.start()

    def wait_step(step, buf_slot):
        def body(i, carry):
            for c in tile_copies(step, buf_slot, i):
                c.wait()
            return carry
        lax.fori_loop(0, per_step, body, 0)

    @pl.when(n == 0)
    def _():
        def body(i, carry):
            start_tile(0, 0, i)
            return carry
        lax.fori_loop(0, per_step, body, 0)

    wait_step(n, slot)
    nxt = (n + 1) % n_steps
    k_own = kn_ref[...]
    v_own = vn_ref[...]
    key_id = lax.broadcasted_iota(jnp.int32, (seq, 1), 0)
    row = lax.broadcasted_iota(jnp.int32, (seq, HEAD_DIM), 0)
    out = jnp.zeros((seq, HEAD_DIM), F32)
    for c in range(seq):
        for t in range(n_tiles):
            start_tile(nxt, 1 - slot, c * n_tiles + t)
        q_row = q_ref[c:c + 1, :]
        s_own = jnp.sum(k_own * q_row, axis=1, keepdims=True) * scale
        s_own = jnp.where(key_id <= c, s_own, NEG_INF)
        s_past = [jnp.sum(kbuf[slot, c * n_tiles + t] * q_row, axis=1, keepdims=True) * scale
                  for t in range(n_tiles)]
        m = jnp.max(s_own, axis=0, keepdims=True)
        for s in s_past:
            m = jnp.maximum(m, jnp.max(s, axis=0, keepdims=True))
        p_own = jnp.exp(s_own - m)
        l = jnp.sum(p_own, axis=0, keepdims=True)
        acc = jnp.sum(p_own * v_own, axis=0, keepdims=True)
        for t, s in enumerate(s_past):
            p = jnp.exp(s - m)
            l = l + jnp.sum(p, axis=0, keepdims=True)
            acc = acc + jnp.sum(p * vbuf[slot, c * n_tiles + t], axis=0, keepdims=True)
        out = jnp.where(row == c, acc / l, out)
    o_ref[...] = out

    @pl.when(n == n_steps - 1)
    def _():
        wait_step(nxt, 1 - slot)


def _moba_sample(qf, k_new, v_new, cache_k, cache_v, phys_flat, layer, *, batch, seq):
    width = qf.shape[1]
    n_heads = width // HEAD_DIM
    page = cache_k.shape[2]
    per_step = seq * 2 * MOBA_TOPK
    new_spec = pl.BlockSpec((seq, HEAD_DIM), lambda n, phys: (n // n_heads, n % n_heads))
    hbm = pl.BlockSpec(memory_space=pl.ANY)
    return pl.pallas_call(
        functools.partial(_moba_sample_kernel, layer=layer, n_heads=n_heads),
        grid_spec=pltpu.PrefetchScalarGridSpec(
            num_scalar_prefetch=1,
            grid=(batch * n_heads,),
            in_specs=[new_spec, new_spec, new_spec, hbm, hbm],
            out_specs=new_spec,
            scratch_shapes=[pltpu.VMEM((2, per_step, page, HEAD_DIM), F32),
                            pltpu.VMEM((2, per_step, page, HEAD_DIM), F32),
                            pltpu.SemaphoreType.DMA((2, 2))],
        ),
        out_shape=jax.ShapeDtypeStruct((batch * seq, width), F32),
        compiler_params=_params("arbitrary"),
        name="moba_sample",
    )(phys_flat, qf, k_new, v_new, cache_k, cache_v)


def _diff_sample_kernel(pt_ref, q_ref, kn_ref, vn_ref, *refs, n_pg, lam_init):
    del pt_ref
    k_refs = refs[:n_pg]
    v_refs = refs[n_pg:2 * n_pg]
    lp_ref, subln_ref, o_ref, qbd_ref, m_ref, l_ref, acc_ref = refs[2 * n_pg:]
    p_id = pl.program_id(1)
    n_p = pl.num_programs(1)
    seq, width = q_ref.shape
    page = k_refs[0].shape[0]
    dd = HEAD_DIM // 2
    n_sub = width // dd
    n_heads = width // HEAD_DIM
    scale = dd ** -0.5
    g = 2 * seq

    @pl.when(p_id == 0)
    def _():
        q = q_ref[...]
        q_rep = jnp.concatenate([q] * n_sub, axis=0)
        r_sub = lax.broadcasted_iota(jnp.int32, q_rep.shape, 0) // seq
        c_sub = lax.broadcasted_iota(jnp.int32, q_rep.shape, 1) // dd
        qbd_ref[...] = jnp.where(r_sub == c_sub, q_rep, 0.0).astype(BF16)
        m_ref[...] = jnp.full(m_ref.shape, NEG_INF, F32)
        l_ref[...] = jnp.zeros(l_ref.shape, F32)
        acc_ref[...] = jnp.zeros(acc_ref.shape, F32)

    def update(s_list, v_list):
        m_old = m_ref[...]
        m_new = m_old
        for s in s_list:
            m_new = jnp.maximum(m_new, jnp.max(s, axis=1, keepdims=True))
        alpha = jnp.exp(m_old - m_new)
        p_list = [jnp.exp(s - m_new).astype(BF16) for s in s_list]
        l_new = alpha * l_ref[...]
        for p in p_list:
            l_new = l_new + jnp.sum(p.astype(F32), axis=1, keepdims=True)
        l_ref[...] = l_new
        m_ref[...] = m_new
        for h in range(n_heads):
            hr = slice(h * g, (h + 1) * g)
            hc = slice(h * HEAD_DIM, (h + 1) * HEAD_DIM)
            acc = alpha[hr, :] * acc_ref[hr, :]
            for p, v in zip(p_list, v_list):
                acc = acc + jnp.dot(p[hr, :], v[:, hc], preferred_element_type=F32)
            acc_ref[hr, :] = acc

    qbd = qbd_ref[...]
    update([_nt_dot(qbd, k_ref[...].astype(BF16)) * scale for k_ref in k_refs],
           [v_ref[...].astype(BF16) for v_ref in v_refs])

    @pl.when(p_id == n_p - 1)
    def _():
        s = _nt_dot(qbd_ref[...], _pad_rows(kn_ref[...], page).astype(BF16)) * scale
        q_pos = lax.broadcasted_iota(jnp.int32, s.shape, 0) % seq
        k_pos = lax.broadcasted_iota(jnp.int32, s.shape, 1)
        update([jnp.where(k_pos <= q_pos, s, NEG_INF)], [_pad_rows(vn_ref[...], page).astype(BF16)])
        o = acc_ref[...] / l_ref[...]
        lam = _diff_lambda(lp_ref) + lam_init
        subln = subln_ref[...]
        for h in range(n_heads):
            o1 = o[h * g:h * g + seq, :]
            o2 = o[h * g + seq:(h + 1) * g, :]
            o_ref[:, h * HEAD_DIM:(h + 1) * HEAD_DIM] = _diff_finish(o1, o2, lam, subln, lam_init)


def _diff_sample(qf, k_new, v_new, cache_k, cache_v, page_table_flat, lp, subln, layer, *,
                 batch, seq, n_pages, lam_init):
    width = qf.shape[1]
    page = cache_k.shape[2]
    rows = (width // (HEAD_DIM // 2)) * seq
    n_pg = _pick(n_pages, (PAGES_PER_STEP, 2, 1))
    new_spec = pl.BlockSpec((seq, width), lambda b, p, pt: (b, 0))

    def kv_spec(j):
        return pl.BlockSpec((None, None, page, width),
                            lambda b, p, pt: (layer, pt[b * n_pages + n_pg * p + j], 0, 0))

    kv_specs = [kv_spec(j) for j in range(n_pg)]
    return pl.pallas_call(
        functools.partial(_diff_sample_kernel, n_pg=n_pg, lam_init=lam_init),
        grid_spec=pltpu.PrefetchScalarGridSpec(
            num_scalar_prefetch=1,
            grid=(batch, n_pages // n_pg),
            in_specs=[new_spec, new_spec, new_spec] + kv_specs + kv_specs +
                     [pl.BlockSpec(lp.shape, lambda b, p, pt: (0, 0)),
                      pl.BlockSpec((1, HEAD_DIM), lambda b, p, pt: (0, 0))],
            out_specs=new_spec,
            scratch_shapes=[pltpu.VMEM((rows, width), BF16), pltpu.VMEM((rows, 1), F32),
                            pltpu.VMEM((rows, 1), F32), pltpu.VMEM((rows, HEAD_DIM), F32)],
        ),
        out_shape=jax.ShapeDtypeStruct((batch * seq, width), F32),
        compiler_params=_params("parallel", "arbitrary"),
        name="diff_sample",
    )(page_table_flat, qf, k_new, v_new, *([cache_k] * n_pg), *([cache_v] * n_pg), lp, subln.reshape(1, HEAD_DIM))


def _layer(i, x, pos, mem_k, mem_v, mem_cols, conv_state, wts, *, batch, seq, paged=None, stack=None):
    width = wts["self_width"]
    is_moba = i % N_MIXERS == 0
    prompt = paged is None
    head_dim = HEAD_DIM if is_moba else HEAD_DIM // 2
    proj = _norm_matmul(x, wts["ln_attn"][i], wts["w_in"], i)
    cos_t, sin_t = _rope_tables(pos, head_dim)
    r = _rope(proj, cos_t, sin_t, seq=seq, width=width, head_dim=head_dim, with_qf=is_moba or not prompt,
              with_bf16=prompt, with_means=is_moba and prompt, stack=stack)
    if is_moba:
        if prompt:
            means = r["means"].reshape(batch, seq // MOBA_BLOCK, width)
            self_out = _moba_prompt(r["qb"], r["q"], r["kb"], r["vb"], means, batch=batch, seq=seq)
        else:
            cache_k, cache_v, pt_flat, n_pages = paged
            means = _page_means(cache_k, pt_flat, i, batch=batch, n_pages=n_pages)
            assert n_pages // 2 >= MOBA_TOPK
            idx = _sample_topk(r["q"], means, batch=batch, seq=seq)[..., :MOBA_TOPK]
            logical = (2 * idx[..., None] + jnp.arange(2, dtype=jnp.int32)).reshape(batch, -1)
            phys = jnp.take_along_axis(pt_flat.reshape(batch, n_pages), logical, axis=1)
            self_out = _moba_sample(r["q"], r["k"], r["v"], cache_k, cache_v, phys.reshape(-1), i,
                                    batch=batch, seq=seq).astype(BF16)
    else:
        j = i // N_MIXERS
        lam_init = 0.8 - 0.6 * math.exp(-0.3 * i)
        lp = wts["diff_lambda"][j]
        subln = wts["diff_subln"][j]
        if prompt:
            self_out = _diff_prompt(r["qb"], r["kb"], r["vb"], lp, subln, batch=batch, seq=seq, lam_init=lam_init)
        else:
            cache_k, cache_v, pt_flat, n_pages = paged
            self_out = _diff_sample(r["q"], r["k"], r["v"], cache_k, cache_v, pt_flat, lp, subln, i,
                                    batch=batch, seq=seq, n_pages=n_pages, lam_init=lam_init).astype(BF16)
    mem_out = _mem_attn(proj, mem_k, mem_v, batch=batch, seq=seq, q_col_block=3 * width // HEAD_DIM,
                        kv_col_blocks=mem_cols)
    x = _matmul_res([self_out, mem_out.astype(BF16)], wts["w_o"], i, x)
    act, new_conv = _ffn_up(x, wts["ln_ffn"][i], wts["w_up"], i, conv_state, wts["conv_w"][i], wts["conv_b"][i],
                            batch=batch, seq=seq)
    x = _matmul_res([act], wts["w_down"], i, x, tm_prefs=(1024, 512, 256, 128, 64),
                    tn_prefs=(256, 128))
    return x, r["k"], r["v"], new_conv


def kernel(x_prompt, x_sample, mem_prompt, cache_k, cache_v, cache_mem_k, cache_mem_v, state_conv, page_table,
           ln_attn, w_in, w_o, diff_lambda, diff_subln, ln_mem, w_mem_kv, ln_ffn, w_up, conv_w, conv_b,
           w_down, ln_final):
    bp, tp, d_model = x_prompt.shape
    bs, ts, _ = x_sample.shape
    depth = w_in.shape[0]
    n_mem = mem_prompt.shape[1]
    mem_width = w_mem_kv.shape[2] // 2
    self_width = d_model - mem_width
    d_ff = w_down.shape[1]
    n_pages = page_table.shape[1]
    page = cache_k.shape[2]
    past = n_pages * page
    assert tp % MOBA_BLOCK == 0 and past % MOBA_BLOCK == 0 and ts <= MOBA_BLOCK

    wts = dict(self_width=self_width, ln_attn=ln_attn, ln_ffn=ln_ffn, diff_lambda=diff_lambda,
               diff_subln=diff_subln, conv_w=conv_w, conv_b=conv_b, w_in=w_in, w_o=w_o, w_up=w_up, w_down=w_down)
    pos_p = jnp.arange(tp, dtype=jnp.int32)
    pos_s = past + jnp.arange(ts, dtype=jnp.int32)
    conv0 = jnp.zeros((bp, CONV_W - 1, d_ff), F32)
    pt_flat = page_table.reshape(-1).astype(jnp.int32)
    paged = (cache_k, cache_v, pt_flat, n_pages)
    mem_heads = mem_width // HEAD_DIM

    xp = x_prompt.reshape(bp * tp, d_model)
    xs = x_sample.reshape(bs * ts, d_model)
    mem_rows = mem_prompt.reshape(bp * n_mem, d_model)
    k_stack = jnp.zeros((depth, bp * tp, self_width), F32)
    v_stack = jnp.zeros((depth, bp * tp, self_width), F32)
    mkv_l, cp_l, ks_l, vs_l, cs_l = [], [], [], [], []
    for i in range(depth):
        mem_kv = _norm_matmul(mem_rows, ln_mem[i], w_mem_kv, i).reshape(bp, n_mem, 2 * mem_width)
        xp, k_stack, v_stack, cp = _layer(i, xp, pos_p, mem_kv, mem_kv, (0, mem_heads), conv0, wts,
                                          batch=bp, seq=tp, stack=(i, k_stack, v_stack))
        xs, k_s, v_s, c_s = _layer(i, xs, pos_s, cache_mem_k[i], cache_mem_v[i], (0, 0), state_conv[i], wts,
                                   batch=bs, seq=ts, paged=paged)
        mkv_l.append(mem_kv)
        cp_l.append(cp)
        ks_l.append(k_s.reshape(bs, ts, self_width))
        vs_l.append(v_s.reshape(bs, ts, self_width))
        cs_l.append(c_s)
    y_prompt = _rmsnorm(xp, ln_final).reshape(bp, tp, d_model)
    y_sample = _rmsnorm(xs, ln_final).reshape(bs, ts, d_model)
    mkv = jnp.stack(mkv_l)
    return (y_prompt, y_sample,
            k_stack.reshape(depth, bp, tp, self_width), v_stack.reshape(depth, bp, tp, self_width),
            mkv[..., :mem_width], mkv[..., mem_width:], jnp.stack(cp_l),
            jnp.stack(ks_l), jnp.stack(vs_l), jnp.stack(cs_l))
```

```python
import functools
import math

import jax
import jax.numpy as jnp
from jax import lax
from jax.experimental import pallas as pl
from jax.experimental.pallas import tpu as pltpu

F32 = jnp.float32
BF16 = jnp.bfloat16

HEAD_DIM = 128
MOBA_BLOCK = 256
MOBA_TOPK = 3
CONV_W = 3
ROPE_THETA = 10000.0
EPS = 1e-6
SUBLN_EPS = 1e-5
N_MIXERS = 2

VMEM_LIMIT_BYTES = 54 * 1024 * 1024
NEG_INF = float("-inf")
LOG2_E = 1.4426950408889634
BF16_SUBLANES = 16
FFN_ROW_TILE = 2048
FFN_ROW_CHUNK = 512
PAGES_PER_STEP = 8
BLOCKS_PER_STEP = 4


def _params(*sem):
    return pltpu.CompilerParams(dimension_semantics=sem, vmem_limit_bytes=VMEM_LIMIT_BYTES)


def _pick(n, prefs):
    for p in prefs:
        if n % p == 0:
            return p
    return n


def _nt_dot(a, b):
    return lax.dot_general(a, b, (((1,), (1,)), ((), ())), preferred_element_type=F32)


def _pad_rows(x, rows):
    if x.shape[0] >= rows:
        return x
    return jnp.concatenate([x, jnp.zeros((rows - x.shape[0],) + x.shape[1:], x.dtype)], axis=0)


def _norm_matmul_kernel(x_ref, g_ref, w_ref, o_ref, h_ref):
    @pl.when(pl.program_id(1) == 0)
    def _():
        x = x_ref[...]
        ms = jnp.mean(x * x, axis=-1, keepdims=True)
        h_ref[...] = (x * lax.rsqrt(ms + EPS) * g_ref[...]).astype(BF16)

    o_ref[...] = jnp.dot(h_ref[...], w_ref[...].astype(BF16), preferred_element_type=F32)


def _norm_matmul(x, g, w_stack, layer):
    m, d = x.shape
    n = w_stack.shape[2]
    tm = _pick(m, (2048, 1024, 512, 256, 128, 64))
    tn = _pick(n, (512, 256, 128))
    x_kwargs = dict(pipeline_mode=pl.Buffered(1)) if tm > 1024 else {}
    return pl.pallas_call(
        _norm_matmul_kernel,
        grid=(m // tm, n // tn),
        in_specs=[
            pl.BlockSpec((tm, d), lambda i, j: (i, 0), **x_kwargs),
            pl.BlockSpec((1, d), lambda i, j: (0, 0)),
            pl.BlockSpec((None, d, tn), lambda i, j: (layer, 0, j)),
        ],
        out_specs=pl.BlockSpec((tm, tn), lambda i, j: (i, j)),
        out_shape=jax.ShapeDtypeStruct((m, n), F32),
        scratch_shapes=[pltpu.VMEM((tm, d), BF16)],
        compiler_params=_params("parallel", "arbitrary"),
        name="norm_matmul",
    )(x, g.reshape(1, d), w_stack)


def _matmul_res_kernel(*refs, n_parts):
    a_refs = refs[:n_parts]
    w_refs = refs[n_parts:2 * n_parts]
    r_ref, o_ref = refs[2 * n_parts:]
    acc = r_ref[...]
    for a_ref, w_ref in zip(a_refs, w_refs):
        acc = acc + jnp.dot(a_ref[...], w_ref[...].astype(BF16), preferred_element_type=F32)
    o_ref[...] = acc


def _matmul_res(a_parts, w_stack, layer, res, *, tm_prefs=(2048, 1024, 512, 256, 128, 64), tn_prefs=(512, 256, 128)):
    m, n = res.shape
    tm = _pick(m, tm_prefs)
    tn = _pick(n, tn_prefs)
    a_kwargs = dict(pipeline_mode=pl.Buffered(1)) if tm > 1024 else {}
    a_specs, w_specs = [], []
    row0 = 0
    for a in a_parts:
        k = a.shape[1]
        assert row0 % k == 0
        a_specs.append(pl.BlockSpec((tm, k), lambda i, j: (i, 0), **a_kwargs))
        w_specs.append(pl.BlockSpec((None, k, tn), lambda i, j, rb=row0 // k: (layer, rb, j)))
        row0 += k
    assert row0 == w_stack.shape[1]
    io_spec = pl.BlockSpec((tm, tn), lambda i, j: (i, j))
    return pl.pallas_call(
        functools.partial(_matmul_res_kernel, n_parts=len(a_parts)),
        grid=(m // tm, n // tn),
        in_specs=a_specs + w_specs + [io_spec],
        out_specs=io_spec,
        out_shape=jax.ShapeDtypeStruct((m, n), F32),
        compiler_params=_params("parallel", "parallel"),
        name="matmul_res",
    )(*a_parts, *([w_stack] * len(a_parts)), res)


def _rmsnorm_kernel(x_ref, g_ref, o_ref):
    x = x_ref[...]
    ms = jnp.mean(x * x, axis=-1, keepdims=True)
    o_ref[...] = x * lax.rsqrt(ms + EPS) * g_ref[...]


def _rmsnorm(x, g):
    m, d = x.shape
    tm = _pick(m, (512, 256, 128, 64))
    return pl.pallas_call(
        _rmsnorm_kernel,
        grid=(m // tm,),
        in_specs=[pl.BlockSpec((tm, d), lambda i: (i, 0)), pl.BlockSpec((1, d), lambda i: (0, 0))],
        out_specs=pl.BlockSpec((tm, d), lambda i: (i, 0)),
        out_shape=jax.ShapeDtypeStruct((m, d), F32),
        compiler_params=_params("parallel"),
        name="rmsnorm",
    )(x, g.reshape(1, d))


def _rope_tables(pos, head_dim):
    half = head_dim // 2
    inv = jnp.exp(-math.log(ROPE_THETA) * jnp.arange(half, dtype=F32) / half)
    ang = pos.astype(F32)[:, None] * inv[None, :]
    cos, sin = jnp.cos(ang), jnp.sin(ang)
    reps = HEAD_DIM // head_dim
    cos_t = jnp.tile(jnp.concatenate([cos, cos], axis=-1), (1, reps))
    sin_t = jnp.tile(jnp.concatenate([-sin, sin], axis=-1), (1, reps))
    return cos_t, sin_t


def _rope_kernel(*refs, head_dim, n_stack_in, with_qf, with_bf16, with_means):
    qb_scale = head_dim ** -0.5 * LOG2_E
    refs = refs[n_stack_in:]
    q_ref, k_ref, v_ref, cos_ref, sin_ref = refs[:5]
    outs = list(refs[5:])
    kf_ref, vf_ref = outs[:2]
    outs = outs[2:]
    qf_ref = outs.pop(0) if with_qf else None
    if with_bf16:
        qb_ref, kb_ref, vb_ref = outs[:3]
        outs = outs[3:]
    mean_ref = outs.pop(0) if with_means else None
    cos = cos_ref[...]
    sin = sin_ref[...]
    half = head_dim // 2
    n_groups = q_ref.shape[1] // HEAD_DIM
    if head_dim != HEAD_DIM:
        lane = lax.broadcasted_iota(jnp.int32, cos.shape, 1)
        first_half = (lane % head_dim) < half

    def rot(x):
        if head_dim == HEAD_DIM:
            partner = pltpu.roll(x, half, 1)
        else:
            partner = jnp.where(first_half, pltpu.roll(x, HEAD_DIM - half, 1), pltpu.roll(x, half, 1))
        return x * cos + partner * sin

    for h in range(n_groups):
        sl = slice(h * HEAD_DIM, (h + 1) * HEAD_DIM)
        qr = rot(q_ref[:, sl])
        kr = rot(k_ref[:, sl])
        kf_ref[:, sl] = kr
        if with_qf:
            qf_ref[:, sl] = qr
        if with_bf16:
            qb_ref[:, sl] = (qr * qb_scale).astype(BF16)
            kb_ref[:, sl] = kr.astype(BF16)
        if with_means:
            mean_ref[:, sl] = jnp.sum(kr, axis=0, keepdims=True) * (1.0 / MOBA_BLOCK)
    v = v_ref[...]
    vf_ref[...] = v
    if with_bf16:
        vb_ref[...] = v.astype(BF16)


def _rope(proj, cos_t, sin_t, *, seq, width, head_dim, with_qf, with_bf16, with_means, stack=None):
    m = proj.shape[0]
    tr = MOBA_BLOCK if seq % MOBA_BLOCK == 0 else seq
    n_t = seq // tr
    if with_means:
        assert tr == MOBA_BLOCK
    row = lambda c: pl.BlockSpec((tr, width), lambda r: (r, c))
    tab = pl.BlockSpec((tr, HEAD_DIM), lambda r: (r % n_t, 0))
    in_specs = [row(0), row(1), row(2), tab, tab]
    args = [proj, proj, proj, cos_t, sin_t]
    aliases = {}
    if stack is None:
        names = ["k", "v"]
        out_shapes = [jax.ShapeDtypeStruct((m, width), F32)] * 2
        out_specs = [row(0)] * 2
        n_stack_in = 0
    else:
        layer, k_stack, v_stack = stack
        names = ["k", "v"]
        out_shapes = [jax.ShapeDtypeStruct(k_stack.shape, F32)] * 2
        out_specs = [pl.BlockSpec((None, tr, width), lambda r: (layer, r, 0))] * 2
        n_stack_in = 2
        in_specs = [pl.BlockSpec(memory_space=pl.ANY)] * 2 + in_specs
        args = [k_stack, v_stack] + args
        aliases = {0: 0, 1: 1}
    if with_qf:
        names.append("q")
        out_shapes.append(jax.ShapeDtypeStruct((m, width), F32))
        out_specs.append(row(0))
    if with_bf16:
        names += ["qb", "kb", "vb"]
        out_shapes += [jax.ShapeDtypeStruct((m, width), BF16)] * 3
        out_specs += [row(0)] * 3
    if with_means:
        names.append("means")
        out_shapes.append(jax.ShapeDtypeStruct((m // tr, 1, width), F32))
        out_specs.append(pl.BlockSpec((None, 1, width), lambda r: (r, 0, 0)))
    outs = pl.pallas_call(
        functools.partial(_rope_kernel, head_dim=head_dim, n_stack_in=n_stack_in, with_qf=with_qf,
                          with_bf16=with_bf16, with_means=with_means),
        grid=(m // tr,),
        in_specs=in_specs,
        out_specs=out_specs,
        out_shape=out_shapes,
        input_output_aliases=aliases,
        compiler_params=_params("parallel"),
        name="rope",
    )(*args)
    return dict(zip(names, outs))


def _softmax_pv(s_blocks, v_ref):
    blk = MOBA_BLOCK
    lanes = HEAD_DIM
    tiles = [s[:, t * lanes:(t + 1) * lanes] for s in s_blocks for t in range(blk // lanes)]
    m_part = tiles[0]
    for t in tiles[1:]:
        m_part = jnp.maximum(m_part, t)
    m = jnp.max(m_part, axis=1, keepdims=True)
    l_part = None
    acc = None
    for kj, s in enumerate(s_blocks):
        p = jnp.exp2(s - m)
        for t in range(blk // lanes):
            pt = p[:, t * lanes:(t + 1) * lanes]
            l_part = pt if l_part is None else l_part + pt
        pv = jnp.dot(p.astype(BF16), v_ref[kj * blk:(kj + 1) * blk, :], preferred_element_type=F32)
        acc = pv if acc is None else acc + pv
    return acc / jnp.sum(l_part, axis=1, keepdims=True)


def _moba_prompt_kernel(qb_ref, qf_ref, k_ref, v_ref, mean_ref, o_ref):
    blk = MOBA_BLOCK
    nq = qb_ref.shape[0] // blk
    means = mean_ref[...]
    causal = lax.broadcasted_iota(jnp.int32, (blk, blk), 1) <= lax.broadcasted_iota(jnp.int32, (blk, blk), 0)
    for qi in range(nq):
        rows = slice(qi * blk, (qi + 1) * blk)
        q = qb_ref[rows, :]
        sel_bias = None
        if qi > MOBA_TOPK:
            gate = lax.dot_general(qf_ref[rows, :], means, (((1,), (1,)), ((), ())),
                                   precision=lax.Precision.HIGHEST, preferred_element_type=F32)
            blk_id = lax.broadcasted_iota(jnp.int32, gate.shape, 1)
            rank = jnp.zeros(gate.shape, F32)
            for m_id in range(qi):
                gm = gate[:, m_id:m_id + 1]
                beats = (gm > gate) | ((gm == gate) & (blk_id > m_id))
                rank = rank + jnp.where(beats, 1.0, 0.0)
            sel_bias = jnp.where(rank < float(MOBA_TOPK), 0.0, NEG_INF)
        s_blocks = []
        for kj in range(qi + 1):
            s = _nt_dot(q, k_ref[kj * blk:(kj + 1) * blk, :])
            if kj == qi:
                s = jnp.where(causal, s, NEG_INF)
            elif sel_bias is not None:
                s = s + sel_bias[:, kj:kj + 1]
            s_blocks.append(s)
        o_ref[rows, :] = _softmax_pv(s_blocks, v_ref).astype(o_ref.dtype)


def _moba_prompt(qb, qf, kb, vb, means, *, batch, seq):
    m, width = qb.shape
    n_heads = width // HEAD_DIM
    nb = means.shape[1]
    spec = pl.BlockSpec((seq, HEAD_DIM), lambda b, h: (b, h))
    return pl.pallas_call(
        _moba_prompt_kernel,
        grid=(batch, n_heads),
        in_specs=[spec, spec, spec, spec, pl.BlockSpec((None, nb, HEAD_DIM), lambda b, h: (b, 0, h))],
        out_specs=spec,
        out_shape=jax.ShapeDtypeStruct((m, width), BF16),
        compiler_params=_params("parallel", "parallel"),
        name="moba_prompt",
    )(qb, qf, kb, vb, means)


def _diff_lambda(lp_ref):
    lp = lp_ref[...]
    s1 = jnp.sum(lp[0:1, :] * lp[1:2, :], axis=1, keepdims=True)
    s2 = jnp.sum(lp[2:3, :] * lp[3:4, :], axis=1, keepdims=True)
    return jnp.exp(s1) - jnp.exp(s2)


def _diff_finish(o1, o2, lam, subln, lam_init):
    o = o1 - lam * o2
    ms = jnp.mean(o * o, axis=-1, keepdims=True)
    return (o * lax.rsqrt(ms + SUBLN_EPS) * subln) * (1.0 - lam_init)


def _diff_prompt_kernel(qb_ref, k_ref, v_ref, lp_ref, subln_ref, o_ref, *, lam_init):
    blk = MOBA_BLOCK
    nq = qb_ref.shape[0] // blk
    dd = HEAD_DIM // 2
    lam = _diff_lambda(lp_ref) + lam_init
    subln = subln_ref[...]
    lane = lax.broadcasted_iota(jnp.int32, (blk, HEAD_DIM), 1)
    r_id = lax.broadcasted_iota(jnp.int32, (2 * blk, blk), 0) % blk
    causal = lax.broadcasted_iota(jnp.int32, (2 * blk, blk), 1) <= r_id
    for qi in range(nq):
        rows = slice(qi * blk, (qi + 1) * blk)
        q = qb_ref[rows, :].astype(F32)
        q_stack = jnp.concatenate([jnp.where(lane < dd, q, 0.0), jnp.where(lane >= dd, q, 0.0)], axis=0).astype(BF16)
        s_blocks = []
        for kj in range(qi + 1):
            s = _nt_dot(q_stack, k_ref[kj * blk:(kj + 1) * blk, :])
            if kj == qi:
                s = jnp.where(causal, s, NEG_INF)
            s_blocks.append(s)
        o = _softmax_pv(s_blocks, v_ref)
        o_ref[rows, :] = _diff_finish(o[:blk], o[blk:], lam, subln, lam_init).astype(o_ref.dtype)


def _diff_prompt(qb, kb, vb, lp, subln, *, batch, seq, lam_init):
    m, width = qb.shape
    n_heads = width // HEAD_DIM
    spec = pl.BlockSpec((seq, HEAD_DIM), lambda b, h: (b, h))
    return pl.pallas_call(
        functools.partial(_diff_prompt_kernel, lam_init=lam_init),
        grid=(batch, n_heads),
        in_specs=[spec, spec, spec,
                  pl.BlockSpec(lp.shape, lambda b, h: (0, 0)),
                  pl.BlockSpec((1, HEAD_DIM), lambda b, h: (0, 0))],
        out_specs=spec,
        out_shape=jax.ShapeDtypeStruct((m, width), BF16),
        compiler_params=_params("parallel", "parallel"),
        name="diff_prompt",
    )(qb, kb, vb, lp, subln.reshape(1, HEAD_DIM))


def _mem_attn_kernel(q_ref, mk_ref, mv_ref, o_ref):
    tq = q_ref.shape[0]
    q = _pad_rows(q_ref[...], BF16_SUBLANES).astype(BF16)
    s = _nt_dot(q, mk_ref[...].astype(BF16)) * (HEAD_DIM ** -0.5)
    m = jnp.max(s, axis=1, keepdims=True)
    p = jnp.exp(s - m)
    l = jnp.sum(p, axis=1, keepdims=True)
    o = jnp.dot(p.astype(BF16), mv_ref[...].astype(BF16), preferred_element_type=F32)
    o_ref[...] = (o / l)[:tq].astype(o_ref.dtype)


def _mem_attn(proj, mem_k, mem_v, *, batch, seq, q_col_block, kv_col_blocks):
    m = proj.shape[0]
    n_mem = mem_k.shape[1]
    n_h = (proj.shape[1] // HEAD_DIM) - q_col_block
    tq = _pick(seq, (2048, 1024, 512, 256, 128, 64, 32, 16, 8))
    nq = seq // tq
    k_spec = pl.BlockSpec((None, n_mem, HEAD_DIM), lambda b, i, h: (b, 0, kv_col_blocks[0] + h))
    v_spec = pl.BlockSpec((None, n_mem, HEAD_DIM), lambda b, i, h: (b, 0, kv_col_blocks[1] + h))
    return pl.pallas_call(
        _mem_attn_kernel,
        grid=(batch, nq, n_h),
        in_specs=[pl.BlockSpec((tq, HEAD_DIM), lambda b, i, h: (b * nq + i, q_col_block + h)), k_spec, v_spec],
        out_specs=pl.BlockSpec((tq, HEAD_DIM), lambda b, i, h: (b * nq + i, h)),
        out_shape=jax.ShapeDtypeStruct((m, n_h * HEAD_DIM), BF16 if tq % BF16_SUBLANES == 0 else F32),
        compiler_params=_params("parallel", "parallel", "parallel"),
        name="mem_attn",
    )(proj, mem_k, mem_v)


def _ffn_up_kernel(x_ref, ln_ref, wg_ref, wa_ref, st_ref, cw_ref, cb_ref, act_ref, nst_ref, h_ref, *, seq, chunk):
    @pl.when(pl.program_id(1) == 0)
    def _():
        x = x_ref[...]
        ms = jnp.mean(x * x, axis=-1, keepdims=True)
        h_ref[...] = (x * lax.rsqrt(ms + EPS) * ln_ref[...]).astype(BF16)

    tm = h_ref.shape[0]
    w = cw_ref[...]
    bias = cb_ref[...]
    wg = wg_ref[...].astype(BF16)
    wa = wa_ref[...].astype(BF16)

    def gate(g0, g1, g2, a):
        c = bias + w[0:1, :] * g2 + w[1:2, :] * g1 + w[2:3, :] * g0
        return (c / (1.0 + jnp.exp(-c)) * a).astype(act_ref.dtype)

    carry = None
    for r in range(tm // chunk):
        rows = slice(r * chunk, (r + 1) * chunk)
        h = h_ref[rows, :]
        g = jnp.dot(h, wg, preferred_element_type=F32)
        a = jnp.dot(h, wa, preferred_element_type=F32)
        g1 = pltpu.roll(g, 1, 0)
        g2 = pltpu.roll(g, 2, 0)
        if seq % chunk == 0:
            if (r * chunk) % seq == 0:
                carry = st_ref[(r * chunk) // seq]
            act_ref[rows, :] = gate(g, g1, g2, a)
            top = slice(r * chunk, r * chunk + BF16_SUBLANES)
            row = lax.broadcasted_iota(jnp.int32, (BF16_SUBLANES, g.shape[1]), 0)
            gt = g[:BF16_SUBLANES]
            t1 = jnp.where(row == 0, carry[1:2, :], pltpu.roll(gt, 1, 0))
            t2 = jnp.where(row == 0, carry[0:1, :], jnp.where(row == 1, carry[1:2, :], pltpu.roll(gt, 2, 0)))
            act_ref[top, :] = gate(gt, t1, t2, a[:BF16_SUBLANES])
            carry = g[chunk - (CONV_W - 1):, :]
            if ((r + 1) * chunk) % seq == 0:
                nst_ref[((r + 1) * chunk) // seq - 1] = carry
        else:
            row = lax.broadcasted_iota(jnp.int32, g.shape, 0)
            for b in range(chunk // seq):
                st = st_ref[r * (chunk // seq) + b]
                g1 = jnp.where(row == b * seq, st[1:2, :], g1)
                g2 = jnp.where(row == b * seq, st[0:1, :], jnp.where(row == b * seq + 1, st[1:2, :], g2))
                nst_ref[r * (chunk // seq) + b] = g[(b + 1) * seq - (CONV_W - 1):(b + 1) * seq, :]
            act_ref[rows, :] = gate(g, g1, g2, a)


def _ffn_up(x, ln, w_stack, layer, state, conv_w, conv_b, *, batch, seq):
    m, d = x.shape
    d_ff = w_stack.shape[2] // 2
    assert seq >= CONV_W - 1
    tm = seq * _pick(batch, (max(1, FFN_ROW_TILE // seq),))
    n_seq = tm // seq
    if seq % BF16_SUBLANES == 0:
        chunk = _pick(seq, (FFN_ROW_CHUNK,))
    else:
        chunk = tm
    assert seq % chunk == 0 or chunk % seq == 0
    tn = _pick(d_ff, (256, 128)) if tm > 1024 else _pick(d_ff, (512, 256, 128))
    nc = d_ff // tn
    x_kwargs = dict(pipeline_mode=pl.Buffered(1)) if tm > 1024 else {}
    return pl.pallas_call(
        functools.partial(_ffn_up_kernel, seq=seq, chunk=chunk),
        grid=(m // tm, nc),
        in_specs=[
            pl.BlockSpec((tm, d), lambda i, j: (i, 0), **x_kwargs),
            pl.BlockSpec((1, d), lambda i, j: (0, 0)),
            pl.BlockSpec((None, d, tn), lambda i, j: (layer, 0, j)),
            pl.BlockSpec((None, d, tn), lambda i, j: (layer, 0, nc + j)),
            pl.BlockSpec((n_seq, CONV_W - 1, tn), lambda i, j: (i, 0, j)),
            pl.BlockSpec((CONV_W, tn), lambda i, j: (0, j)),
            pl.BlockSpec((1, tn), lambda i, j: (0, j)),
        ],
        out_specs=[
            pl.BlockSpec((tm, tn), lambda i, j: (i, j)),
            pl.BlockSpec((n_seq, CONV_W - 1, tn), lambda i, j: (i, 0, j)),
        ],
        out_shape=[
            jax.ShapeDtypeStruct((m, d_ff), BF16),
            jax.ShapeDtypeStruct((batch, CONV_W - 1, d_ff), F32),
        ],
        scratch_shapes=[pltpu.VMEM((tm, d), BF16)],
        compiler_params=_params("parallel", "arbitrary"),
        name="ffn_up",
    )(x, ln.reshape(1, d), w_stack, w_stack, state, conv_w, conv_b.reshape(1, d_ff))


def _page_mean_kernel(pt_ref, *refs):
    del pt_ref
    o_ref = refs[-1]
    for g in range(o_ref.shape[0]):
        s = jnp.sum(refs[2 * g][...], axis=0, keepdims=True) + jnp.sum(refs[2 * g + 1][...], axis=0, keepdims=True)
        o_ref[g:g + 1, :] = s * (1.0 / MOBA_BLOCK)


def _page_means(cache_k, page_table_flat, layer, *, batch, n_pages):
    page, width = cache_k.shape[2:]
    assert MOBA_BLOCK == 2 * page
    n_blocks = n_pages // 2
    nbs = _pick(n_blocks, (BLOCKS_PER_STEP, 2, 1))
    n_steps = n_blocks // nbs

    def k_spec(j):
        return pl.BlockSpec((None, None, page, width),
                            lambda b, n, pt: (layer, pt[b * n_pages + 2 * nbs * n + j], 0, 0))

    out = pl.pallas_call(
        _page_mean_kernel,
        grid_spec=pltpu.PrefetchScalarGridSpec(
            num_scalar_prefetch=1,
            grid=(batch, n_steps),
            in_specs=[k_spec(j) for j in range(2 * nbs)],
            out_specs=pl.BlockSpec((None, None, nbs, width), lambda b, n, pt: (b, n, 0, 0)),
        ),
        out_shape=jax.ShapeDtypeStruct((batch, n_steps, nbs, width), F32),
        compiler_params=_params("parallel", "parallel"),
        name="page_means",
    )(page_table_flat, *([cache_k] * (2 * nbs)))
    return out.reshape(batch, n_blocks, width)


def _topk_kernel(q_ref, mean_ref, o_ref):
    n_heads = q_ref.shape[1] // HEAD_DIM
    n_blocks = mean_ref.shape[0]
    for h in range(n_heads):
        sl = slice(h * HEAD_DIM, (h + 1) * HEAD_DIM)
        gate = lax.dot_general(q_ref[:, sl], mean_ref[:, sl], (((1,), (1,)), ((), ())),
                               precision=lax.Precision.HIGHEST, preferred_element_type=F32)
        lane = lax.broadcasted_iota(jnp.int32, gate.shape, 1)
        out_lane = lax.broadcasted_iota(jnp.int32, (gate.shape[0], HEAD_DIM), 1)
        out = jnp.zeros((gate.shape[0], HEAD_DIM), jnp.int32)
        for r in range(MOBA_TOPK):
            mx = jnp.max(gate, axis=1, keepdims=True)
            idx = jnp.min(jnp.where(gate == mx, lane, n_blocks), axis=1, keepdims=True)
            out = jnp.where(out_lane == r, idx, out)
            gate = jnp.where(lane == idx, NEG_INF, gate)
        o_ref[h] = out


def _sample_topk(qf, means, *, batch, seq):
    width = qf.shape[1]
    n_heads = width // HEAD_DIM
    n_blocks = means.shape[1]
    return pl.pallas_call(
        _topk_kernel,
        grid=(batch,),
        in_specs=[pl.BlockSpec((seq, width), lambda b: (b, 0)),
                  pl.BlockSpec((None, n_blocks, width), lambda b: (b, 0, 0))],
        out_specs=pl.BlockSpec((None, n_heads, seq, HEAD_DIM), lambda b: (b, 0, 0, 0)),
        out_shape=jax.ShapeDtypeStruct((batch, n_heads, seq, HEAD_DIM), jnp.int32),
        compiler_params=_params("parallel"),
        name="sample_topk",
    )(qf, means)


def _moba_sample_kernel(phys_ref, q_ref, kn_ref, vn_ref, ck_hbm, cv_hbm, o_ref, kbuf, vbuf, sem, *, layer, n_heads):
    n = pl.program_id(0)
    n_steps = pl.num_programs(0)
    seq = q_ref.shape[0]
    n_tiles = 2 * MOBA_TOPK
    per_step = seq * n_tiles
    scale = HEAD_DIM ** -0.5
    slot = n % 2

    def tile_copies(step, buf_slot, i):
        page_id = phys_ref[step * per_step + i]
        lane0 = pl.multiple_of((step % n_heads) * HEAD_DIM, HEAD_DIM)
        kc = pltpu.make_async_copy(ck_hbm.at[layer, page_id, :, pl.ds(lane0, HEAD_DIM)], kbuf.at[buf_slot, i],
                                   sem.at[buf_slot, 0])
        vc = pltpu.make_async_copy(cv_hbm.at[layer, page_id, :, pl.ds(lane0, HEAD_DIM)], vbuf.at[buf_slot, i],
                                   sem.at[buf_slot, 1])
        return kc, vc

    def start_tile(step, buf_slot, i):
        for c in tile_copies(step, buf_slot, i):
            c.start()

    def wait_step(step, buf_slot):
        def body(i, carry):
            for c in tile_copies(step, buf_slot, i):
                c.wait()
            return carry
        lax.fori_loop(0, per_step, body, 0)

    @pl.when(n == 0)
    def _():
        def body(i, carry):
            start_tile(0, 0, i)
            return carry
        lax.fori_loop(0, per_step, body, 0)

    wait_step(n, slot)
    nxt = (n + 1) % n_steps
    k_own = kn_ref[...]
    v_own = vn_ref[...]
    key_id = lax.broadcasted_iota(jnp.int32, (seq, 1), 0)
    row = lax.broadcasted_iota(jnp.int32, (seq, HEAD_DIM), 0)
    out = jnp.zeros((seq, HEAD_DIM), F32)
    for c in range(seq):
        for t in range(n_tiles):
            start_tile(nxt, 1 - slot, c * n_tiles + t)
        q_row = q_ref[c:c + 1, :]
        s_own = jnp.sum(k_own * q_row, axis=1, keepdims=True) * scale
        s_own = jnp.where(key_id <= c, s_own, NEG_INF)
        s_past = [jnp.sum(kbuf[slot, c * n_tiles + t] * q_row, axis=1, keepdims=True) * scale
                  for t in range(n_tiles)]
        m = jnp.max(s_own, axis=0, keepdims=True)
        for s in s_past:
            m = jnp.maximum(m, jnp.max(s, axis=0, keepdims=True))
        p_own = jnp.exp(s_own - m)
        l = jnp.sum(p_own, axis=0, keepdims=True)
        acc = jnp.sum(p_own * v_own, axis=0, keepdims=True)
        for t, s in enumerate(s_past):
            p = jnp.exp(s - m)
            l = l + jnp.sum(p, axis=0, keepdims=True)
            acc = acc + jnp.sum(p * vbuf[slot, c * n_tiles + t], axis=0, keepdims=True)
        out = jnp.where(row == c, acc / l, out)
    o_ref[...] = out

    @pl.when(n == n_steps - 1)
    def _():
        wait_step(nxt, 1 - slot)


def _moba_sample(qf, k_new, v_new, cache_k, cache_v, phys_flat, layer, *, batch, seq):
    width = qf.shape[1]
    n_heads = width // HEAD_DIM
    page = cache_k.shape[2]
    per_step = seq * 2 * MOBA_TOPK
    new_spec = pl.BlockSpec((seq, HEAD_DIM), lambda n, phys: (n // n_heads, n % n_heads))
    hbm = pl.BlockSpec(memory_space=pl.ANY)
    return pl.pallas_call(
        functools.partial(_moba_sample_kernel, layer=layer, n_heads=n_heads),
        grid_spec=pltpu.PrefetchScalarGridSpec(
            num_scalar_prefetch=1,
            grid=(batch * n_heads,),
            in_specs=[new_spec, new_spec, new_spec, hbm, hbm],
            out_specs=new_spec,
            scratch_shapes=[pltpu.VMEM((2, per_step, page, HEAD_DIM), F32),
                            pltpu.VMEM((2, per_step, page, HEAD_DIM), F32),
                            pltpu.SemaphoreType.DMA((2, 2))],
        ),
        out_shape=jax.ShapeDtypeStruct((batch * seq, width), F32),
        compiler_params=_params("arbitrary"),
        name="moba_sample",
    )(phys_flat, qf, k_new, v_new, cache_k, cache_v)


def _diff_sample_kernel(pt_ref, q_ref, kn_ref, vn_ref, *refs, n_pg, lam_init):
    del pt_ref
    k_refs = refs[:n_pg]
    v_refs = refs[n_pg:2 * n_pg]
    lp_ref, subln_ref, o_ref, qbd_ref, m_ref, l_ref, acc_ref = refs[2 * n_pg:]
    p_id = pl.program_id(1)
    n_p = pl.num_programs(1)
    seq, width = q_ref.shape
    page = k_refs[0].shape[0]
    dd = HEAD_DIM // 2
    n_sub = width // dd
    n_heads = width // HEAD_DIM
    scale = dd ** -0.5
    g = 2 * seq

    @pl.when(p_id == 0)
    def _():
        q = q_ref[...]
        q_rep = jnp.concatenate([q] * n_sub, axis=0)
        r_sub = lax.broadcasted_iota(jnp.int32, q_rep.shape, 0) // seq
        c_sub = lax.broadcasted_iota(jnp.int32, q_rep.shape, 1) // dd
        qbd_ref[...] = jnp.where(r_sub == c_sub, q_rep, 0.0).astype(BF16)
        m_ref[...] = jnp.full(m_ref.shape, NEG_INF, F32)
        l_ref[...] = jnp.zeros(l_ref.shape, F32)
        acc_ref[...] = jnp.zeros(acc_ref.shape, F32)

    def update(s_list, v_list):
        m_old = m_ref[...]
        m_new = m_old
        for s in s_list:
            m_new = jnp.maximum(m_new, jnp.max(s, axis=1, keepdims=True))
        alpha = jnp.exp(m_old - m_new)
        p_list = [jnp.exp(s - m_new).astype(BF16) for s in s_list]
        l_new = alpha * l_ref[...]
        for p in p_list:
            l_new = l_new + jnp.sum(p.astype(F32), axis=1, keepdims=True)
        l_ref[...] = l_new
        m_ref[...] = m_new
        for h in range(n_heads):
            hr = slice(h * g, (h + 1) * g)
            hc = slice(h * HEAD_DIM, (h + 1) * HEAD_DIM)
            acc = alpha[hr, :] * acc_ref[hr, :]
            for p, v in zip(p_list, v_list):
                acc = acc + jnp.dot(p[hr, :], v[:, hc], preferred_element_type=F32)
            acc_ref[hr, :] = acc

    qbd = qbd_ref[...]
    update([_nt_dot(qbd, k_ref[...].astype(BF16)) * scale for k_ref in k_refs],
           [v_ref[...].astype(BF16) for v_ref in v_refs])

    @pl.when(p_id == n_p - 1)
    def _():
        s = _nt_dot(qbd_ref[...], _pad_rows(kn_ref[...], page).astype(BF16)) * scale
        q_pos = lax.broadcasted_iota(jnp.int32, s.shape, 0) % seq
        k_pos = lax.broadcasted_iota(jnp.int32, s.shape, 1)
        update([jnp.where(k_pos <= q_pos, s, NEG_INF)], [_pad_rows(vn_ref[...], page).astype(BF16)])
        o = acc_ref[...] / l_ref[...]
        lam = _diff_lambda(lp_ref) + lam_init
        subln = subln_ref[...]
        for h in range(n_heads):
            o1 = o[h * g:h * g + seq, :]
            o2 = o[h * g + seq:(h + 1) * g, :]
            o_ref[:, h * HEAD_DIM:(h + 1) * HEAD_DIM] = _diff_finish(o1, o2, lam, subln, lam_init)


def _diff_sample(qf, k_new, v_new, cache_k, cache_v, page_table_flat, lp, subln, layer, *,
                 batch, seq, n_pages, lam_init):
    width = qf.shape[1]
    page = cache_k.shape[2]
    rows = (width // (HEAD_DIM // 2)) * seq
    n_pg = _pick(n_pages, (PAGES_PER_STEP, 2, 1))
    new_spec = pl.BlockSpec((seq, width), lambda b, p, pt: (b, 0))

    def kv_spec(j):
        return pl.BlockSpec((None, None, page, width),
                            lambda b, p, pt: (layer, pt[b * n_pages + n_pg * p + j], 0, 0))

    kv_specs = [kv_spec(j) for j in range(n_pg)]
    return pl.pallas_call(
        functools.partial(_diff_sample_kernel, n_pg=n_pg, lam_init=lam_init),
        grid_spec=pltpu.PrefetchScalarGridSpec(
            num_scalar_prefetch=1,
            grid=(batch, n_pages // n_pg),
            in_specs=[new_spec, new_spec, new_spec] + kv_specs + kv_specs +
                     [pl.BlockSpec(lp.shape, lambda b, p, pt: (0, 0)),
                      pl.BlockSpec((1, HEAD_DIM), lambda b, p, pt: (0, 0))],
            out_specs=new_spec,
            scratch_shapes=[pltpu.VMEM((rows, width), BF16), pltpu.VMEM((rows, 1), F32),
                            pltpu.VMEM((rows, 1), F32), pltpu.VMEM((rows, HEAD_DIM), F32)],
        ),
        out_shape=jax.ShapeDtypeStruct((batch * seq, width), F32),
        compiler_params=_params("parallel", "arbitrary"),
        name="diff_sample",
    )(page_table_flat, qf, k_new, v_new, *([cache_k] * n_pg), *([cache_v] * n_pg), lp, subln.reshape(1, HEAD_DIM))


def _layer(i, x, pos, mem_k, mem_v, mem_cols, conv_state, wts, *, batch, seq, paged=None, stack=None):
    width = wts["self_width"]
    is_moba = i % N_MIXERS == 0
    prompt = paged is None
    head_dim = HEAD_DIM if is_moba else HEAD_DIM // 2
    proj = _norm_matmul(x, wts["ln_attn"][i], wts["w_in"], i)
    cos_t, sin_t = _rope_tables(pos, head_dim)
    r = _rope(proj, cos_t, sin_t, seq=seq, width=width, head_dim=head_dim, with_qf=is_moba or not prompt,
              with_bf16=prompt, with_means=is_moba and prompt, stack=stack)
    if is_moba:
        if prompt:
            means = r["means"].reshape(batch, seq // MOBA_BLOCK, width)
            self_out = _moba_prompt(r["qb"], r["q"], r["kb"], r["vb"], means, batch=batch, seq=seq)
        else:
            cache_k, cache_v, pt_flat, n_pages = paged
            means = _page_means(cache_k, pt_flat, i, batch=batch, n_pages=n_pages)
            assert n_pages // 2 >= MOBA_TOPK
            idx = _sample_topk(r["q"], means, batch=batch, seq=seq)[..., :MOBA_TOPK]
            logical = (2 * idx[..., None] + jnp.arange(2, dtype=jnp.int32)).reshape(batch, -1)
            phys = jnp.take_along_axis(pt_flat.reshape(batch, n_pages), logical, axis=1)
            self_out = _moba_sample(r["q"], r["k"], r["v"], cache_k, cache_v, phys.reshape(-1), i,
                                    batch=batch, seq=seq).astype(BF16)
    else:
        j = i // N_MIXERS
        lam_init = 0.8 - 0.6 * math.exp(-0.3 * i)
        lp = wts["diff_lambda"][j]
        subln = wts["diff_subln"][j]
        if prompt:
            self_out = _diff_prompt(r["qb"], r["kb"], r["vb"], lp, subln, batch=batch, seq=seq, lam_init=lam_init)
        else:
            cache_k, cache_v, pt_flat, n_pages = paged
            self_out = _diff_sample(r["q"], r["k"], r["v"], cache_k, cache_v, pt_flat, lp, subln, i,
                                    batch=batch, seq=seq, n_pages=n_pages, lam_init=lam_init).astype(BF16)
    mem_out = _mem_attn(proj, mem_k, mem_v, batch=batch, seq=seq, q_col_block=3 * width // HEAD_DIM,
                        kv_col_blocks=mem_cols)
    x = _matmul_res([self_out, mem_out.astype(BF16)], wts["w_o"], i, x)
    act, new_conv = _ffn_up(x, wts["ln_ffn"][i], wts["w_up"], i, conv_state, wts["conv_w"][i], wts["conv_b"][i],
                            batch=batch, seq=seq)
    x = _matmul_res([act], wts["w_down"], i, x, tm_prefs=(1024, 512, 256, 128, 64),
                    tn_prefs=(256, 128))
    return x, r["k"], r["v"], new_conv


def kernel(x_prompt, x_sample, mem_prompt, cache_k, cache_v, cache_mem_k, cache_mem_v, state_conv, page_table,
           ln_attn, w_in, w_o, diff_lambda, diff_subln, ln_mem, w_mem_kv, ln_ffn, w_up, conv_w, conv_b,
           w_down, ln_final):
    bp, tp, d_model = x_prompt.shape
    bs, ts, _ = x_sample.shape
    depth = w_in.shape[0]
    n_mem = mem_prompt.shape[1]
    mem_width = w_mem_kv.shape[2] // 2
    self_width = d_model - mem_width
    d_ff = w_down.shape[1]
    n_pages = page_table.shape[1]
    page = cache_k.shape[2]
    past = n_pages * page
    assert tp % MOBA_BLOCK == 0 and past % MOBA_BLOCK == 0 and ts <= MOBA_BLOCK

    wts = dict(self_width=self_width, ln_attn=ln_attn, ln_ffn=ln_ffn, diff_lambda=diff_lambda,
               diff_subln=diff_subln, conv_w=conv_w, conv_b=conv_b, w_in=w_in, w_o=w_o, w_up=w_up, w_down=w_down)
    pos_p = jnp.arange(tp, dtype=jnp.int32)
    pos_s = past + jnp.arange(ts, dtype=jnp.int32)
    conv0 = jnp.zeros((bp, CONV_W - 1, d_ff), F32)
    pt_flat = page_table.reshape(-1).astype(jnp.int32)
    paged = (cache_k, cache_v, pt_flat, n_pages)
    mem_heads = mem_width // HEAD_DIM

    xp = x_prompt.reshape(bp * tp, d_model)
    xs = x_sample.reshape(bs * ts, d_model)
    mem_rows = mem_prompt.reshape(bp * n_mem, d_model)
    k_stack = jnp.zeros((depth, bp * tp, self_width), F32)
    v_stack = jnp.zeros((depth, bp * tp, self_width), F32)
    mkv_l, cp_l, ks_l, vs_l, cs_l = [], [], [], [], []
    for i in range(depth):
        mem_kv = _norm_matmul(mem_rows, ln_mem[i], w_mem_kv, i).reshape(bp, n_mem, 2 * mem_width)
        xp, k_stack, v_stack, cp = _layer(i, xp, pos_p, mem_kv, mem_kv, (0, mem_heads), conv0, wts,
                                          batch=bp, seq=tp, stack=(i, k_stack, v_stack))
        xs, k_s, v_s, c_s = _layer(i, xs, pos_s, cache_mem_k[i], cache_mem_v[i], (0, 0), state_conv[i], wts,
                                   batch=bs, seq=ts, paged=paged)
        mkv_l.append(mem_kv)
        cp_l.append(cp)
        ks_l.append(k_s.reshape(bs, ts, self_width))
        vs_l.append(v_s.reshape(bs, ts, self_width))
        cs_l.append(c_s)
    y_prompt = _rmsnorm(xp, ln_final).reshape(bp, tp, d_model)
    y_sample = _rmsnorm(xs, ln_final).reshape(bs, ts, d_model)
    mkv = jnp.stack(mkv_l)
    return (y_prompt, y_sample,
            k_stack.reshape(depth, bp, tp, self_width), v_stack.reshape(depth, bp, tp, self_width),
            mkv[..., :mem_width], mkv[..., mem_width:], jnp.stack(cp_l),
            jnp.stack(ks_l), jnp.stack(vs_l), jnp.stack(cs_l))
```
